```python
import jax, jax.numpy as jnp
from jax import lax
import numpy as np

D_MODEL = 2048
BATCH = 8
SEQ = 2048
DEPTH = 2

GRID_W = 64
BLOCK = 128
HEAD_DIM = 128
ATTN_WIDTH = D_MODEL // 2
N_Q_HEADS = ATTN_WIDTH // HEAD_DIM
N_KV_HEADS = N_Q_HEADS // 4
ROPE_THETA = 10000.0
ROPE_PAIRS = HEAD_DIM // 2
ROPE_FREQ_PER_AXIS = ROPE_PAIRS // 2
CONV_CH = D_MODEL // 2
CONV_WIDTH = 31
CONV_PAD = CONV_WIDTH // 2
SG_CH = D_MODEL // 2
SG_GROUP_CH = 128
SG_GROUPS = SG_CH // SG_GROUP_CH
SG_CHUNK = 128
N_BRANCH = 3
D_FF = -(-8 * D_MODEL // (3 * 256)) * 256

Q_COLS = N_Q_HEADS * HEAD_DIM
KV_COLS = N_KV_HEADS * HEAD_DIM
IN_WIDTHS = [Q_COLS, KV_COLS, KV_COLS, 2 * CONV_CH, 2 * SG_CH, N_BRANCH * D_MODEL]
IN_COLS = int(sum(IN_WIDTHS))
IN_SPLITS = [int(s) for s in np.cumsum(IN_WIDTHS)[:-1]]

kernel_name = "hybrid_gqa_conformer_sgu_encoder"


def rms_norm(x, g, eps=1e-6):
    xf = x.astype(jnp.float32)
    y = xf * lax.rsqrt(jnp.mean(xf * xf, axis=-1, keepdims=True) + eps)
    return (y * g.astype(jnp.float32)).astype(x.dtype)


def layer_norm(x, g, b, eps=1e-5):
    xf = x.astype(jnp.float32)
    mu = jnp.mean(xf, axis=-1, keepdims=True)
    xc = xf - mu
    y = xc * lax.rsqrt(jnp.mean(xc * xc, axis=-1, keepdims=True) + eps)
    return (y * g.astype(jnp.float32) + b.astype(jnp.float32)).astype(x.dtype)


def axial_rope(n):
    rows = n // GRID_W
    row = jnp.repeat(jnp.arange(rows, dtype=jnp.float32), GRID_W)
    col = jnp.tile(jnp.arange(GRID_W, dtype=jnp.float32), rows)
    inv = ROPE_THETA ** (-jnp.arange(ROPE_FREQ_PER_AXIS, dtype=jnp.float32) / ROPE_FREQ_PER_AXIS)
    ang = jnp.concatenate([row[:, None] * inv, col[:, None] * inv], axis=-1)
    return jnp.cos(ang), jnp.sin(ang)


def apply_rope(x, cos, sin):
    xf = x.astype(jnp.float32)
    c = cos[None, :, None, :]
    s = sin[None, :, None, :]
    x1, x2 = xf[..., :ROPE_PAIRS], xf[..., ROPE_PAIRS:]
    return jnp.concatenate([x1 * c - x2 * s, x2 * c + x1 * s], axis=-1).astype(x.dtype)


def blocked_gqa(q, k, v):
    B, S = q.shape[0], q.shape[1]
    nb = S // BLOCK
    grp = N_Q_HEADS // N_KV_HEADS
    qb = q.reshape(B, nb, BLOCK, N_KV_HEADS, grp, HEAD_DIM).transpose(1, 0, 2, 3, 4, 5)
    scale = HEAD_DIM ** -0.5

    def one_block(q_blk):
        s = jnp.einsum('bqhgd,bkhd->bhgqk', q_blk, k).astype(jnp.float32) * scale
        p = jax.nn.softmax(s, axis=-1).astype(v.dtype)
        return jnp.einsum('bhgqk,bkhd->bqhgd', p, v)

    o = lax.map(one_block, qb)
    return o.transpose(1, 0, 2, 3, 4, 5).reshape(B, S, N_Q_HEADS * HEAD_DIM)


def conformer_conv(glu_in, w_dw, b_dw, ln_g, ln_b):
    a, g = jnp.split(glu_in, 2, axis=-1)
    z = a * jax.nn.sigmoid(g)
    z = lax.conv_general_dilated(
        z, w_dw, window_strides=(1,), padding=((CONV_PAD, CONV_PAD),),
        dimension_numbers=('NWC', 'WIO', 'NWC'), feature_group_count=CONV_CH) + b_dw
    z = layer_norm(z, ln_g, ln_b)
    return jax.nn.silu(z)


def spatial_gating(uv, ln_g, ln_b, w_s, b_s):
    B, S = uv.shape[0], uv.shape[1]
    nc = S // SG_CHUNK
    u, v = jnp.split(jax.nn.gelu(uv), 2, axis=-1)
    v = layer_norm(v.reshape(B, S, SG_GROUPS, SG_GROUP_CH),
                   ln_g.reshape(SG_GROUPS, SG_GROUP_CH), ln_b.reshape(SG_GROUPS, SG_GROUP_CH))
    v = v.reshape(B, nc, SG_CHUNK, SG_GROUPS, SG_GROUP_CH)
    v = jnp.einsum('gpq,bnqgc->bnpgc', w_s, v) + b_s.T[None, None, :, :, None]
    return u * v.reshape(B, S, SG_CH)


def setup_inputs(seed: int = 0) -> dict:
    key = jax.random.key(seed)
    ks = jax.random.split(key, 24)
    f32 = jnp.float32

    def nrm(k, shape, scale):
        return jax.random.normal(k, shape, f32) * scale

    L = DEPTH
    return {
        "x": nrm(ks[0], (BATCH, SEQ, D_MODEL), 1.0),
        "g_mix": 1.0 + nrm(ks[1], (L, D_MODEL), 0.02),
        "w_in": nrm(ks[2], (L, D_MODEL, IN_COLS), D_MODEL ** -0.5),
        "b_gate": nrm(ks[3], (L, N_BRANCH * D_MODEL), 0.01),
        "q_norm_g": 1.0 + nrm(ks[4], (L, HEAD_DIM), 0.02),
        "k_norm_g": 1.0 + nrm(ks[5], (L, HEAD_DIM), 0.02),
        "w_attn_o": nrm(ks[6], (L, Q_COLS, D_MODEL), Q_COLS ** -0.5),
        "w_dw": nrm(ks[7], (L, CONV_WIDTH, 1, CONV_CH), CONV_WIDTH ** -0.5),
        "b_dw": nrm(ks[8], (L, CONV_CH), 0.01),
        "conv_ln_g": 1.0 + nrm(ks[9], (L, CONV_CH), 0.02),
        "conv_ln_b": nrm(ks[10], (L, CONV_CH), 0.01),
        "w_conv_o": nrm(ks[11], (L, CONV_CH, D_MODEL), CONV_CH ** -0.5),
        "sg_ln_g": 1.0 + nrm(ks[12], (L, SG_CH), 0.02),
        "sg_ln_b": nrm(ks[13], (L, SG_CH), 0.01),
        "w_s": nrm(ks[14], (L, SG_GROUPS, SG_CHUNK, SG_CHUNK), SG_CHUNK ** -0.5),
        "b_s": 1.0 + nrm(ks[15], (L, SG_GROUPS, SG_CHUNK), 0.01),
        "w_sg_o": nrm(ks[16], (L, SG_CH, D_MODEL), SG_CH ** -0.5),
        "w_out": nrm(ks[17], (L, D_MODEL, D_MODEL), D_MODEL ** -0.5),
        "g_ffn": 1.0 + nrm(ks[18], (L, D_MODEL), 0.02),
        "w_ff_gate": nrm(ks[19], (L, D_MODEL, D_FF), D_MODEL ** -0.5),
        "w_ff_up": nrm(ks[20], (L, D_MODEL, D_FF), D_MODEL ** -0.5),
        "w_ff_down": nrm(ks[21], (L, D_FF, D_MODEL), D_FF ** -0.5),
        "g_final": 1.0 + nrm(ks[22], (D_MODEL,), 0.02),
    }


def reference(x, g_mix, w_in, b_gate, q_norm_g, k_norm_g, w_attn_o, w_dw, b_dw,
              conv_ln_g, conv_ln_b, w_conv_o, sg_ln_g, sg_ln_b, w_s, b_s, w_sg_o,
              w_out, g_ffn, w_ff_gate, w_ff_up, w_ff_down, g_final):
    B, S, _ = x.shape
    cos, sin = axial_rope(S)
    for l in range(DEPTH):
        h = rms_norm(x, g_mix[l])
        proj = h @ w_in[l]
        q, k, v, conv_in, sg_in, gate_logits = jnp.split(proj, IN_SPLITS, axis=-1)

        q = rms_norm(q.reshape(B, S, N_Q_HEADS, HEAD_DIM), q_norm_g[l])
        k = rms_norm(k.reshape(B, S, N_KV_HEADS, HEAD_DIM), k_norm_g[l])
        v = v.reshape(B, S, N_KV_HEADS, HEAD_DIM)
        q = apply_rope(q, cos, sin)
        k = apply_rope(k, cos, sin)
        y_attn = blocked_gqa(q, k, v) @ w_attn_o[l]

        y_conv = conformer_conv(conv_in, w_dw[l], b_dw[l], conv_ln_g[l], conv_ln_b[l]) @ w_conv_o[l]

        y_sg = spatial_gating(sg_in, sg_ln_g[l], sg_ln_b[l], w_s[l], b_s[l]) @ w_sg_o[l]

        gates = jax.nn.sigmoid((gate_logits + b_gate[l]).reshape(B, S, N_BRANCH, D_MODEL))
        merged = gates[:, :, 0] * y_attn + gates[:, :, 1] * y_conv + gates[:, :, 2] * y_sg
        x = x + merged @ w_out[l]

        hf = rms_norm(x, g_ffn[l])
        x = x + (jax.nn.silu(hf @ w_ff_gate[l]) * (hf @ w_ff_up[l])) @ w_ff_down[l]
    return rms_norm(x, g_final)
```

```python
import functools

import jax
import jax.numpy as jnp
from jax import lax
from jax.experimental import pallas as pl
from jax.experimental.pallas import tpu as pltpu

D_MODEL = 2048
GRID_W = 64
HEAD_DIM = 128
N_Q_HEADS = 8
N_KV_HEADS = 2
Q_PER_KV = N_Q_HEADS // N_KV_HEADS
ROPE_THETA = 10000.0
ROPE_PAIRS = HEAD_DIM // 2
ROPE_FREQ_PER_AXIS = ROPE_PAIRS // 2
CONV_CH = 1024
CONV_WIDTH = 31
CONV_PAD = CONV_WIDTH // 2
SG_CH = 1024
SG_GROUP_CH = 128
SG_GROUPS = SG_CH // SG_GROUP_CH
SG_CHUNK = 128
N_BRANCH = 3
Q_COLS = N_Q_HEADS * HEAD_DIM
KV_COLS = N_KV_HEADS * HEAD_DIM
QKV_COLS = Q_COLS + 2 * KV_COLS
RMS_EPS = 1e-6
LN_EPS = 1e-5

V7X_VMEM_BYTES = 64 * 1024 * 1024
V7X_SUBLANES = 8
CONV_HALO = -(-CONV_PAD // V7X_SUBLANES) * V7X_SUBLANES

BF16 = jnp.bfloat16
F32 = jnp.float32


def _params(vmem_mib, semantics):
    return pltpu.CompilerParams(
        dimension_semantics=semantics, vmem_limit_bytes=vmem_mib * 1024 * 1024)


def _resident(shape):
    return pl.BlockSpec(shape, lambda *_: (0,) * len(shape), pipeline_mode=pl.Buffered(1))


def _rms(xf, gain):
    return xf * lax.rsqrt(jnp.mean(xf * xf, axis=-1, keepdims=True) + RMS_EPS) * gain


def _layer_norm(xf, gain, bias):
    mu = jnp.mean(xf, axis=-1, keepdims=True)
    xc = xf - mu
    return xc * lax.rsqrt(jnp.mean(xc * xc, axis=-1, keepdims=True) + LN_EPS) * gain + bias


def _dot(a, b):
    return jnp.dot(a, b, preferred_element_type=F32)


def _qkv_kernel(x_ref, g_ref, w_ref, qg_ref, kg_ref, cos_ref, sin_ref, h_ref, qkv_ref):
    h = _rms(x_ref[...], g_ref[...]).astype(BF16)
    h_ref[...] = h
    acc = _dot(h, w_ref[...])
    cos = cos_ref[...]
    sin = sin_ref[...]
    scale = HEAD_DIM ** -0.5
    for head in range(N_Q_HEADS + N_KV_HEADS):
        lo = head * HEAD_DIM
        is_q = head < N_Q_HEADS
        gain = qg_ref[...] if is_q else kg_ref[...]
        y = _rms(acc[:, lo:lo + HEAD_DIM], gain)
        y = y * cos + pltpu.roll(y, ROPE_PAIRS, axis=1) * sin
        if is_q:
            y = y * scale
        qkv_ref[:, lo:lo + HEAD_DIM] = y.astype(BF16)
    v_lo = Q_COLS + KV_COLS
    qkv_ref[:, v_lo:] = acc[:, v_lo:].astype(BF16)


def _qkv_call(x2, g, w, qg, kg, cos2, sin2, seq, tm=512):
    t = x2.shape[0]
    pos_blocks = seq // tm
    return pl.pallas_call(
        _qkv_kernel,
        grid=(t // tm,),
        in_specs=[
            pl.BlockSpec((tm, D_MODEL), lambda i: (i, 0)),
            _resident((1, D_MODEL)),
            _resident((D_MODEL, QKV_COLS)),
            _resident((1, HEAD_DIM)),
            _resident((1, HEAD_DIM)),
            pl.BlockSpec((tm, HEAD_DIM), lambda i: (i % pos_blocks, 0)),
            pl.BlockSpec((tm, HEAD_DIM), lambda i: (i % pos_blocks, 0)),
        ],
        out_specs=[
            pl.BlockSpec((tm, D_MODEL), lambda i: (i, 0)),
            pl.BlockSpec((tm, QKV_COLS), lambda i: (i, 0)),
        ],
        out_shape=[
            jax.ShapeDtypeStruct((t, D_MODEL), BF16),
            jax.ShapeDtypeStruct((t, QKV_COLS), BF16),
        ],
        compiler_params=_params(48, ("parallel",)),
        name="rms_qkv_rope",
    )(x2, g, w, qg, kg, cos2, sin2)


def _glu_kernel(h_ref, wa_ref, wg_ref, z_ref):
    h = h_ref[...]
    a = _dot(h, wa_ref[...])
    g = _dot(h, wg_ref[...])
    z_ref[...] = (a * jax.nn.sigmoid(g)).astype(BF16)


def _glu_call(h, wa, wg, tm=512):
    t = h.shape[0]
    return pl.pallas_call(
        _glu_kernel,
        grid=(t // tm,),
        in_specs=[
            pl.BlockSpec((tm, D_MODEL), lambda i: (i, 0)),
            _resident((D_MODEL, CONV_CH)),
            _resident((D_MODEL, CONV_CH)),
        ],
        out_specs=pl.BlockSpec((tm, CONV_CH), lambda i: (i, 0)),
        out_shape=jax.ShapeDtypeStruct((t, CONV_CH), BF16),
        compiler_params=_params(48, ("parallel",)),
        name="conv_glu_in",
    )(h, wa, wg)


def _sgu_in_kernel(h_ref, wu_ref, wv_ref, g_ref, b_ref, u_ref, v_ref):
    h = h_ref[...]
    u_ref[...] = jax.nn.gelu(_dot(h, wu_ref[...])).astype(BF16)
    v = jax.nn.gelu(_dot(h, wv_ref[...]))
    for grp in range(SG_GROUPS):
        lo = grp * SG_GROUP_CH
        hi = lo + SG_GROUP_CH
        v_ref[:, lo:hi] = _layer_norm(v[:, lo:hi], g_ref[:, lo:hi], b_ref[:, lo:hi]).astype(BF16)


def _sgu_in_call(h, wu, wv, ln_g, ln_b, tm=512):
    t = h.shape[0]
    return pl.pallas_call(
        _sgu_in_kernel,
        grid=(t // tm,),
        in_specs=[
            pl.BlockSpec((tm, D_MODEL), lambda i: (i, 0)),
            _resident((D_MODEL, SG_CH)),
            _resident((D_MODEL, SG_CH)),
            _resident((1, SG_CH)),
            _resident((1, SG_CH)),
        ],
        out_specs=[
            pl.BlockSpec((tm, SG_CH), lambda i: (i, 0)),
            pl.BlockSpec((tm, SG_CH), lambda i: (i, 0)),
        ],
        out_shape=[
            jax.ShapeDtypeStruct((t, SG_CH), BF16),
            jax.ShapeDtypeStruct((t, SG_CH), BF16),
        ],
        compiler_params=_params(48, ("parallel",)),
        name="sgu_in",
    )(h, wu, wv, ln_g, ln_b)


def _gates_kernel(h_ref, w_ref, b_ref, o_ref):
    o_ref[...] = jax.nn.sigmoid(_dot(h_ref[...], w_ref[...]) + b_ref[...]).astype(BF16)


def _gates_call(h, w, b, tm=1024, tn=1024):
    t = h.shape[0]
    n = w.shape[1]
    return pl.pallas_call(
        _gates_kernel,
        grid=(t // tm, n // tn),
        in_specs=[
            pl.BlockSpec((tm, D_MODEL), lambda i, j: (i, 0)),
            pl.BlockSpec((D_MODEL, tn), lambda i, j: (0, j)),
            pl.BlockSpec((1, tn), lambda i, j: (0, j)),
        ],
        out_specs=pl.BlockSpec((tm, tn), lambda i, j: (i, j)),
        out_shape=jax.ShapeDtypeStruct((t, n), BF16),
        compiler_params=_params(48, ("parallel", "arbitrary")),
        name="branch_gates",
    )(h, w, b)


def _attn_kernel(q_ref, k_ref, v_ref, o_ref):
    k = k_ref[...]
    v = v_ref[...]
    for grp in range(Q_PER_KV):
        lo = grp * HEAD_DIM
        q = q_ref[:, lo:lo + HEAD_DIM]
        s = lax.dot_general(q, k, (((1,), (1,)), ((), ())), preferred_element_type=F32)
        p = jnp.exp(s - jnp.max(s, axis=-1, keepdims=True))
        denom = jnp.sum(p, axis=-1, keepdims=True)
        o = _dot(p.astype(BF16), v) / denom
        o_ref[:, lo:lo + HEAD_DIM] = o.astype(BF16)


def _attn_call(qkv, batch, seq, tq=512):
    t = qkv.shape[0]
    q_blocks = seq // tq
    grp_cols = Q_PER_KV * HEAD_DIM
    k_col0 = Q_COLS // HEAD_DIM
    v_col0 = (Q_COLS + KV_COLS) // HEAD_DIM
    return pl.pallas_call(
        _attn_kernel,
        grid=(batch, N_KV_HEADS, q_blocks),
        in_specs=[
            pl.BlockSpec((tq, grp_cols), lambda b, h, i: (b * q_blocks + i, h)),
            pl.BlockSpec((seq, HEAD_DIM), lambda b, h, i: (b, k_col0 + h)),
            pl.BlockSpec((seq, HEAD_DIM), lambda b, h, i: (b, v_col0 + h)),
        ],
        out_specs=pl.BlockSpec((tq, grp_cols), lambda b, h, i: (b * q_blocks + i, h)),
        out_shape=jax.ShapeDtypeStruct((t, Q_COLS), BF16),
        compiler_params=_params(48, ("parallel", "parallel", "arbitrary")),
        name="gqa_attention",
    )(qkv, qkv, qkv)


def _conv_kernel(z_ref, w_ref, b_ref, g_ref, beta_ref, o_ref, pad_ref, *, seq, rows):
    zeros = jnp.zeros((CONV_HALO, CONV_CH), F32)
    pad_ref[0:CONV_HALO, :] = zeros
    pad_ref[CONV_HALO + seq:, :] = zeros
    pad_ref[CONV_HALO:CONV_HALO + seq, :] = z_ref[...].astype(F32)

    def tile(i, carry):
        r0 = pl.multiple_of(i * rows, rows)
        acc = jnp.zeros((rows, CONV_CH), F32)
        win_ref = pad_ref.at[pl.ds(r0, rows + 2 * CONV_HALO), :]
        for tap in range(CONV_WIDTH):
            off = CONV_HALO - CONV_PAD + tap
            acc = acc + win_ref[off:off + rows, :] * w_ref[tap:tap + 1, :]
        y = _layer_norm(acc + b_ref[...], g_ref[...], beta_ref[...])
        o_ref[pl.ds(r0, rows), :] = (y * jax.nn.sigmoid(y)).astype(BF16)
        return carry

    lax.fori_loop(0, seq // rows, tile, 0)


def _conv_call(z, w_dw, b_dw, ln_g, ln_b, batch, seq, rows=32):
    t = z.shape[0]
    return pl.pallas_call(
        functools.partial(_conv_kernel, seq=seq, rows=rows),
        grid=(batch,),
        in_specs=[
            pl.BlockSpec((seq, CONV_CH), lambda b: (b, 0)),
            _resident((CONV_WIDTH, CONV_CH)),
            _resident((1, CONV_CH)),
            _resident((1, CONV_CH)),
            _resident((1, CONV_CH)),
        ],
        out_specs=pl.BlockSpec((seq, CONV_CH), lambda b: (b, 0)),
        out_shape=jax.ShapeDtypeStruct((t, CONV_CH), BF16),
        scratch_shapes=[pltpu.VMEM((seq + 2 * CONV_HALO, CONV_CH), F32)],
        compiler_params=_params(48, ("parallel",)),
        name="conv_ln_swish",
    )(z, w_dw, b_dw, ln_g, ln_b)


def _sgu_mix_kernel(u_ref, v_ref, ws_ref, bst_ref, o_ref, *, chunks):
    for grp in range(SG_GROUPS):
        lo = grp * SG_GROUP_CH
        w = ws_ref[grp]
        bias = bst_ref[:, grp:grp + 1]
        for c in range(chunks):
            r0 = c * SG_CHUNK
            mixed = _dot(w, v_ref[r0:r0 + SG_CHUNK, lo:lo + SG_GROUP_CH]) + bias
            u = u_ref[r0:r0 + SG_CHUNK, lo:lo + SG_GROUP_CH].astype(F32)
            o_ref[r0:r0 + SG_CHUNK, lo:lo + SG_GROUP_CH] = (u * mixed).astype(BF16)


def _sgu_mix_call(u, v, ws, bst, tm=1024):
    t = u.shape[0]
    return pl.pallas_call(
        functools.partial(_sgu_mix_kernel, chunks=tm // SG_CHUNK),
        grid=(t // tm,),
        in_specs=[
            pl.BlockSpec((tm, SG_CH), lambda i: (i, 0)),
            pl.BlockSpec((tm, SG_CH), lambda i: (i, 0)),
            _resident((SG_GROUPS, SG_CHUNK, SG_CHUNK)),
            _resident((SG_CHUNK, SG_GROUPS)),
        ],
        out_specs=pl.BlockSpec((tm, SG_CH), lambda i: (i, 0)),
        out_shape=jax.ShapeDtypeStruct((t, SG_CH), BF16),
        compiler_params=_params(48, ("parallel",)),
        name="sgu_mix",
    )(u, v, ws, bst)


def _merge_kernel(a_ref, c_ref, s_ref, gate_ref, x_ref, wa_ref, wc_ref, ws_ref, wo_ref, o_ref):
    merged = gate_ref[:, 0:D_MODEL].astype(F32) * _dot(a_ref[...], wa_ref[...])
    merged = merged + gate_ref[:, D_MODEL:2 * D_MODEL].astype(F32) * _dot(c_ref[...], wc_ref[...])
    merged = merged + gate_ref[:, 2 * D_MODEL:].astype(F32) * _dot(s_ref[...], ws_ref[...])
    o_ref[...] = x_ref[...] + _dot(merged.astype(BF16), wo_ref[...])


def _merge_call(attn, conv, sgu, gates, x2, wa, wc, ws, wo, tm=256):
    t = x2.shape[0]
    half = D_MODEL // 2
    return pl.pallas_call(
        _merge_kernel,
        grid=(t // tm,),
        in_specs=[
            pl.BlockSpec((tm, half), lambda i: (i, 0)),
            pl.BlockSpec((tm, half), lambda i: (i, 0)),
            pl.BlockSpec((tm, half), lambda i: (i, 0)),
            pl.BlockSpec((tm, N_BRANCH * D_MODEL), lambda i: (i, 0)),
            pl.BlockSpec((tm, D_MODEL), lambda i: (i, 0)),
            _resident((half, D_MODEL)),
            _resident((half, D_MODEL)),
            _resident((half, D_MODEL)),
            _resident((D_MODEL, D_MODEL)),
        ],
        out_specs=pl.BlockSpec((tm, D_MODEL), lambda i: (i, 0)),
        out_shape=jax.ShapeDtypeStruct((t, D_MODEL), F32),
        compiler_params=_params(56, ("parallel",)),
        name="merge_out_proj",
    )(attn, conv, sgu, gates, x2, wa, wc, ws, wo)


def _ffn_kernel(x_ref, g_ref, wg_ref, wu_ref, wd_ref, gf_ref, o_ref, h_ref, acc_ref, *, final_norm):
    j = pl.program_id(1)

    @pl.when(j == 0)
    def _():
        h_ref[...] = _rms(x_ref[...], g_ref[...]).astype(BF16)
        acc_ref[...] = jnp.zeros_like(acc_ref)

    h = h_ref[...]
    a = _dot(h, wg_ref[...])
    b = _dot(h, wu_ref[...])
    mid = (a * jax.nn.sigmoid(a) * b).astype(BF16)
    acc_ref[...] += _dot(mid, wd_ref[...])

    @pl.when(j == pl.num_programs(1) - 1)
    def _():
        y = x_ref[...] + acc_ref[...]
        if final_norm:
            y = _rms(y, gf_ref[...])
        o_ref[...] = y


def _ffn_call(x2, g, wg, wu, wd, g_final, final_norm, tm=512, tf=512):
    t = x2.shape[0]
    d_ff = wg.shape[1]
    return pl.pallas_call(
        functools.partial(_ffn_kernel, final_norm=final_norm),
        grid=(t // tm, d_ff // tf),
        in_specs=[
            pl.BlockSpec((tm, D_MODEL), lambda i, j: (i, 0)),
            _resident((1, D_MODEL)),
            pl.BlockSpec((D_MODEL, tf), lambda i, j: (0, j)),
            pl.BlockSpec((D_MODEL, tf), lambda i, j: (0, j)),
            pl.BlockSpec((tf, D_MODEL), lambda i, j: (j, 0)),
            _resident((1, D_MODEL)),
        ],
        out_specs=pl.BlockSpec((tm, D_MODEL), lambda i, j: (i, 0)),
        out_shape=jax.ShapeDtypeStruct((t, D_MODEL), F32),
        scratch_shapes=[pltpu.VMEM((tm, D_MODEL), BF16), pltpu.VMEM((tm, D_MODEL), F32)],
        compiler_params=_params(56, ("parallel", "arbitrary")),
        name="swiglu_ffn",
    )(x2, g, wg, wu, wd, g_final)


def _rope_tables(seq):
    pos = jnp.arange(seq, dtype=jnp.int32)
    row = (pos // GRID_W).astype(F32)
    col = (pos % GRID_W).astype(F32)
    inv = ROPE_THETA ** (-jnp.arange(ROPE_FREQ_PER_AXIS, dtype=F32) / ROPE_FREQ_PER_AXIS)
    ang = jnp.concatenate([row[:, None] * inv, col[:, None] * inv], axis=-1)
    cos, sin = jnp.cos(ang), jnp.sin(ang)
    return jnp.concatenate([cos, cos], axis=-1), jnp.concatenate([-sin, sin], axis=-1)


def kernel(x, g_mix, w_in, b_gate, q_norm_g, k_norm_g, w_attn_o, w_dw, b_dw, conv_ln_g, conv_ln_b, w_conv_o, sg_ln_g, sg_ln_b, w_s, b_s, w_sg_o, w_out, g_ffn, w_ff_gate, w_ff_up, w_ff_down, g_final):
    batch, seq, _ = x.shape
    depth = w_in.shape[0]
    cos2, sin2 = _rope_tables(seq)
    x2 = x.reshape(batch * seq, D_MODEL)
    row = lambda p: p.reshape(1, -1)
    c0 = QKV_COLS
    c1 = c0 + CONV_CH
    c2 = c1 + CONV_CH
    c3 = c2 + SG_CH
    c4 = c3 + SG_CH

    for l in range(depth):
        w = w_in[l]
        h, qkv = _qkv_call(x2, row(g_mix[l]), w[:, :c0].astype(BF16), row(q_norm_g[l]),
                           row(k_norm_g[l]), cos2, sin2, seq)
        z = _glu_call(h, w[:, c0:c1].astype(BF16), w[:, c1:c2].astype(BF16))
        u, v = _sgu_in_call(h, w[:, c2:c3].astype(BF16), w[:, c3:c4].astype(BF16),
                            row(sg_ln_g[l]), row(sg_ln_b[l]))
        gates = _gates_call(h, w[:, c4:].astype(BF16), row(b_gate[l]))
        attn = _attn_call(qkv, batch, seq)
        conv = _conv_call(z, w_dw[l].reshape(CONV_WIDTH, CONV_CH), row(b_dw[l]),
                          row(conv_ln_g[l]), row(conv_ln_b[l]), batch, seq)
        sgu = _sgu_mix_call(u, v, w_s[l].astype(BF16), b_s[l].T)
        x2 = _merge_call(attn, conv, sgu, gates, x2, w_attn_o[l].astype(BF16),
                         w_conv_o[l].astype(BF16), w_sg_o[l].astype(BF16), w_out[l].astype(BF16))
        x2 = _ffn_call(x2, row(g_ffn[l]), w_ff_gate[l].astype(BF16), w_ff_up[l].astype(BF16),
                       w_ff_down[l].astype(BF16), row(g_final), final_norm=(l == depth - 1))
    return x2.reshape(batch, seq, D_MODEL)
```

```python
import functools

import jax
import jax.numpy as jnp
from jax import lax
from jax.experimental import pallas as pl
from jax.experimental.pallas import tpu as pltpu

D_MODEL = 2048
GRID_W = 64
HEAD_DIM = 128
N_Q_HEADS = 8
N_KV_HEADS = 2
Q_PER_KV = N_Q_HEADS // N_KV_HEADS
ROPE_THETA = 10000.0
ROPE_PAIRS = HEAD_DIM // 2
ROPE_FREQ_PER_AXIS = ROPE_PAIRS // 2
CONV_CH = 1024
CONV_WIDTH = 31
CONV_PAD = CONV_WIDTH // 2
SG_CH = 1024
SG_GROUP_CH = 128
SG_GROUPS = SG_CH // SG_GROUP_CH
SG_CHUNK = 128
N_BRANCH = 3
Q_COLS = N_Q_HEADS * HEAD_DIM
KV_COLS = N_KV_HEADS * HEAD_DIM
QKV_COLS = Q_COLS + 2 * KV_COLS
RMS_EPS = 1e-6
LN_EPS = 1e-5

V7X_VMEM_BYTES = 64 * 1024 * 1024
V7X_SUBLANES = 8
V7X_LANES = 128
CONV_HALO = -(-CONV_PAD // V7X_SUBLANES) * V7X_SUBLANES

BF16 = jnp.bfloat16
F32 = jnp.float32


def _params(vmem_mib, semantics):
    return pltpu.CompilerParams(
        dimension_semantics=semantics, vmem_limit_bytes=vmem_mib * 1024 * 1024)


def _resident(shape):
    return pl.BlockSpec(shape, lambda *_: (0,) * len(shape), pipeline_mode=pl.Buffered(1))


def _rms(xf, gain):
    return xf * lax.rsqrt(jnp.mean(xf * xf, axis=-1, keepdims=True) + RMS_EPS) * gain


def _layer_norm(xf, gain, bias):
    mu = jnp.mean(xf, axis=-1, keepdims=True)
    xc = xf - mu
    return xc * lax.rsqrt(jnp.mean(xc * xc, axis=-1, keepdims=True) + LN_EPS) * gain + bias


def _dot(a, b):
    return jnp.dot(a, b, preferred_element_type=F32)


def _qkv_kernel(x_ref, g_ref, w_ref, qg_ref, kg_ref, cos_ref, sin_ref, h_ref, qkv_ref):
    h = _rms(x_ref[...], g_ref[...]).astype(BF16)
    h_ref[...] = h
    acc = _dot(h, w_ref[...])
    cos = cos_ref[...]
    sin = sin_ref[...]
    scale = HEAD_DIM ** -0.5
    for head in range(N_Q_HEADS + N_KV_HEADS):
        lo = head * HEAD_DIM
        is_q = head < N_Q_HEADS
        gain = qg_ref[...] if is_q else kg_ref[...]
        y = _rms(acc[:, lo:lo + HEAD_DIM], gain)
        y = y * cos + pltpu.roll(y, ROPE_PAIRS, axis=1) * sin
        if is_q:
            y = y * scale
        qkv_ref[:, lo:lo + HEAD_DIM] = y.astype(BF16)
    v_lo = Q_COLS + KV_COLS
    qkv_ref[:, v_lo:] = acc[:, v_lo:].astype(BF16)


def _qkv_call(x2, g, w, qg, kg, cos2, sin2, seq, tm=512):
    t = x2.shape[0]
    pos_blocks = seq // tm
    return pl.pallas_call(
        _qkv_kernel,
        grid=(t // tm,),
        in_specs=[
            pl.BlockSpec((tm, D_MODEL), lambda i: (i, 0)),
            _resident((1, D_MODEL)),
            _resident((D_MODEL, QKV_COLS)),
            _resident((1, HEAD_DIM)),
            _resident((1, HEAD_DIM)),
            pl.BlockSpec((tm, HEAD_DIM), lambda i: (i % pos_blocks, 0)),
            pl.BlockSpec((tm, HEAD_DIM), lambda i: (i % pos_blocks, 0)),
        ],
        out_specs=[
            pl.BlockSpec((tm, D_MODEL), lambda i: (i, 0)),
            pl.BlockSpec((tm, QKV_COLS), lambda i: (i, 0)),
        ],
        out_shape=[
            jax.ShapeDtypeStruct((t, D_MODEL), BF16),
            jax.ShapeDtypeStruct((t, QKV_COLS), BF16),
        ],
        compiler_params=_params(48, ("parallel",)),
        name="rms_qkv_rope",
    )(x2, g, w, qg, kg, cos2, sin2)


def _glu_kernel(h_ref, wa_ref, wg_ref, z_ref):
    h = h_ref[...]
    a = _dot(h, wa_ref[...])
    g = _dot(h, wg_ref[...])
    z_ref[...] = (a * jax.nn.sigmoid(g)).astype(BF16)


def _glu_call(h, wa, wg, tm=512):
    t = h.shape[0]
    return pl.pallas_call(
        _glu_kernel,
        grid=(t // tm,),
        in_specs=[
            pl.BlockSpec((tm, D_MODEL), lambda i: (i, 0)),
            _resident((D_MODEL, CONV_CH)),
            _resident((D_MODEL, CONV_CH)),
        ],
        out_specs=pl.BlockSpec((tm, CONV_CH), lambda i: (i, 0)),
        out_shape=jax.ShapeDtypeStruct((t, CONV_CH), BF16),
        compiler_params=_params(48, ("parallel",)),
        name="conv_glu_in",
    )(h, wa, wg)


def _sgu_in_kernel(h_ref, wu_ref, wv_ref, g_ref, b_ref, u_ref, v_ref):
    h = h_ref[...]
    u_ref[...] = jax.nn.gelu(_dot(h, wu_ref[...])).astype(BF16)
    v = jax.nn.gelu(_dot(h, wv_ref[...]))
    for grp in range(SG_GROUPS):
        lo = grp * SG_GROUP_CH
        hi = lo + SG_GROUP_CH
        v_ref[:, lo:hi] = _layer_norm(v[:, lo:hi], g_ref[:, lo:hi], b_ref[:, lo:hi]).astype(BF16)


def _sgu_in_call(h, wu, wv, ln_g, ln_b, tm=512):
    t = h.shape[0]
    return pl.pallas_call(
        _sgu_in_kernel,
        grid=(t // tm,),
        in_specs=[
            pl.BlockSpec((tm, D_MODEL), lambda i: (i, 0)),
            _resident((D_MODEL, SG_CH)),
            _resident((D_MODEL, SG_CH)),
            _resident((1, SG_CH)),
            _resident((1, SG_CH)),
        ],
        out_specs=[
            pl.BlockSpec((tm, SG_CH), lambda i: (i, 0)),
            pl.BlockSpec((tm, SG_CH), lambda i: (i, 0)),
        ],
        out_shape=[
            jax.ShapeDtypeStruct((t, SG_CH), BF16),
            jax.ShapeDtypeStruct((t, SG_CH), BF16),
        ],
        compiler_params=_params(48, ("parallel",)),
        name="sgu_in",
    )(h, wu, wv, ln_g, ln_b)


def _gates_kernel(h_ref, w_ref, b_ref, o_ref):
    o_ref[...] = jax.nn.sigmoid(_dot(h_ref[...], w_ref[...]) + b_ref[...]).astype(BF16)


def _gates_call(h, w, b, tm=1024, tn=1024):
    t = h.shape[0]
    n = w.shape[1]
    return pl.pallas_call(
        _gates_kernel,
        grid=(t // tm, n // tn),
        in_specs=[
            pl.BlockSpec((tm, D_MODEL), lambda i, j: (i, 0)),
            pl.BlockSpec((D_MODEL, tn), lambda i, j: (0, j)),
            pl.BlockSpec((1, tn), lambda i, j: (0, j)),
        ],
        out_specs=pl.BlockSpec((tm, tn), lambda i, j: (i, j)),
        out_shape=jax.ShapeDtypeStruct((t, n), BF16),
        compiler_params=_params(48, ("parallel", "arbitrary")),
        name="branch_gates",
    )(h, w, b)


def _attn_kernel(q_ref, k_ref, v_ref, o_ref):
    k = k_ref[...]
    v = v_ref[...]
    for grp in range(Q_PER_KV):
        lo = grp * HEAD_DIM
        q = q_ref[:, lo:lo + HEAD_DIM]
        s = lax.dot_general(q, k, (((1,), (1,)), ((), ())), preferred_element_type=F32)
        p = jnp.exp(s - jnp.max(s, axis=-1, keepdims=True))
        denom = jnp.sum(p, axis=-1, keepdims=True)
        o = _dot(p.astype(BF16), v) / denom
        o_ref[:, lo:lo + HEAD_DIM] = o.astype(BF16)


def _attn_call(qkv, batch, seq, tq=512):
    t = qkv.shape[0]
    q_blocks = seq // tq
    grp_cols = Q_PER_KV * HEAD_DIM
    k_col0 = Q_COLS // HEAD_DIM
    v_col0 = (Q_COLS + KV_COLS) // HEAD_DIM
    return pl.pallas_call(
        _attn_kernel,
        grid=(batch, N_KV_HEADS, q_blocks),
        in_specs=[
            pl.BlockSpec((tq, grp_cols), lambda b, h, i: (b * q_blocks + i, h)),
            pl.BlockSpec((seq, HEAD_DIM), lambda b, h, i: (b, k_col0 + h)),
            pl.BlockSpec((seq, HEAD_DIM), lambda b, h, i: (b, v_col0 + h)),
        ],
        out_specs=pl.BlockSpec((tq, grp_cols), lambda b, h, i: (b * q_blocks + i, h)),
        out_shape=jax.ShapeDtypeStruct((t, Q_COLS), BF16),
        compiler_params=_params(48, ("parallel", "parallel", "arbitrary")),
        name="gqa_attention",
    )(qkv, qkv, qkv)


def _conv_kernel(z_ref, w_ref, b_ref, g_ref, beta_ref, o_ref, pad_ref, conv_ref, *, seq, rows):
    zeros = jnp.zeros((CONV_HALO, CONV_CH), F32)
    pad_ref[0:CONV_HALO, :] = zeros
    pad_ref[CONV_HALO + seq:, :] = zeros
    pad_ref[CONV_HALO:CONV_HALO + seq, :] = z_ref[...].astype(F32)
    win_rows = rows + 2 * CONV_HALO
    lane_blocks = CONV_CH // V7X_LANES
    row_groups = 2 * CONV_HALO // V7X_SUBLANES

    def tile(i, carry):
        r0 = pl.multiple_of(i * rows, rows)
        for cb in range(lane_blocks):
            lanes = slice(cb * V7X_LANES, (cb + 1) * V7X_LANES)
            win = pad_ref[pl.ds(r0, win_rows), lanes]
            acc = jnp.zeros((rows, V7X_LANES), F32)
            for sub in range(V7X_SUBLANES):
                shifted = win if sub == 0 else pltpu.roll(win, win_rows - sub, axis=0)
                for grp in range(row_groups):
                    tap = V7X_SUBLANES * grp + sub - (CONV_HALO - CONV_PAD)
                    if 0 <= tap < CONV_WIDTH:
                        lo = V7X_SUBLANES * grp
                        acc = acc + shifted[lo:lo + rows, :] * w_ref[tap:tap + 1, lanes]
            conv_ref[:, lanes] = acc + b_ref[:, lanes]
        y = _layer_norm(conv_ref[...], g_ref[...], beta_ref[...])
        o_ref[pl.ds(r0, rows), :] = (y * jax.nn.sigmoid(y)).astype(BF16)
        return carry

    lax.fori_loop(0, seq // rows, tile, 0)


def _conv_call(z, w_dw, b_dw, ln_g, ln_b, batch, seq, rows=64):
    t = z.shape[0]
    return pl.pallas_call(
        functools.partial(_conv_kernel, seq=seq, rows=rows),
        grid=(batch,),
        in_specs=[
            pl.BlockSpec((seq, CONV_CH), lambda b: (b, 0)),
            _resident((CONV_WIDTH, CONV_CH)),
            _resident((1, CONV_CH)),
            _resident((1, CONV_CH)),
            _resident((1, CONV_CH)),
        ],
        out_specs=pl.BlockSpec((seq, CONV_CH), lambda b: (b, 0)),
        out_shape=jax.ShapeDtypeStruct((t, CONV_CH), BF16),
        scratch_shapes=[pltpu.VMEM((seq + 2 * CONV_HALO, CONV_CH), F32),
                        pltpu.VMEM((rows, CONV_CH), F32)],
        compiler_params=_params(48, ("parallel",)),
        name="conv_ln_swish",
    )(z, w_dw, b_dw, ln_g, ln_b)


def _sgu_mix_kernel(u_ref, v_ref, ws_ref, bst_ref, o_ref, *, chunks):
    for grp in range(SG_GROUPS):
        lo = grp * SG_GROUP_CH
        w = ws_ref[grp]
        bias = bst_ref[:, grp:grp + 1]
        for c in range(chunks):
            r0 = c * SG_CHUNK
            mixed = _dot(w, v_ref[r0:r0 + SG_CHUNK, lo:lo + SG_GROUP_CH]) + bias
            u = u_ref[r0:r0 + SG_CHUNK, lo:lo + SG_GROUP_CH].astype(F32)
            o_ref[r0:r0 + SG_CHUNK, lo:lo + SG_GROUP_CH] = (u * mixed).astype(BF16)


def _sgu_mix_call(u, v, ws, bst, tm=1024):
    t = u.shape[0]
    return pl.pallas_call(
        functools.partial(_sgu_mix_kernel, chunks=tm // SG_CHUNK),
        grid=(t // tm,),
        in_specs=[
            pl.BlockSpec((tm, SG_CH), lambda i: (i, 0)),
            pl.BlockSpec((tm, SG_CH), lambda i: (i, 0)),
            _resident((SG_GROUPS, SG_CHUNK, SG_CHUNK)),
            _resident((SG_CHUNK, SG_GROUPS)),
        ],
        out_specs=pl.BlockSpec((tm, SG_CH), lambda i: (i, 0)),
        out_shape=jax.ShapeDtypeStruct((t, SG_CH), BF16),
        compiler_params=_params(48, ("parallel",)),
        name="sgu_mix",
    )(u, v, ws, bst)


def _merge_kernel(a_ref, c_ref, s_ref, gate_ref, x_ref, wa_ref, wc_ref, ws_ref, wo_ref, o_ref):
    merged = gate_ref[:, 0:D_MODEL].astype(F32) * _dot(a_ref[...], wa_ref[...])
    merged = merged + gate_ref[:, D_MODEL:2 * D_MODEL].astype(F32) * _dot(c_ref[...], wc_ref[...])
    merged = merged + gate_ref[:, 2 * D_MODEL:].astype(F32) * _dot(s_ref[...], ws_ref[...])
    o_ref[...] = x_ref[...] + _dot(merged.astype(BF16), wo_ref[...])


def _merge_call(attn, conv, sgu, gates, x2, wa, wc, ws, wo, tm=256):
    t = x2.shape[0]
    half = D_MODEL // 2
    return pl.pallas_call(
        _merge_kernel,
        grid=(t // tm,),
        in_specs=[
            pl.BlockSpec((tm, half), lambda i: (i, 0)),
            pl.BlockSpec((tm, half), lambda i: (i, 0)),
            pl.BlockSpec((tm, half), lambda i: (i, 0)),
            pl.BlockSpec((tm, N_BRANCH * D_MODEL), lambda i: (i, 0)),
            pl.BlockSpec((tm, D_MODEL), lambda i: (i, 0)),
            _resident((half, D_MODEL)),
            _resident((half, D_MODEL)),
            _resident((half, D_MODEL)),
            _resident((D_MODEL, D_MODEL)),
        ],
        out_specs=pl.BlockSpec((tm, D_MODEL), lambda i: (i, 0)),
        out_shape=jax.ShapeDtypeStruct((t, D_MODEL), F32),
        compiler_params=_params(56, ("parallel",)),
        name="merge_out_proj",
    )(attn, conv, sgu, gates, x2, wa, wc, ws, wo)


def _ffn_kernel(x_ref, g_ref, wg_ref, wu_ref, wd_ref, gf_ref, o_ref, h_ref, *, final_norm):
    j = pl.program_id(1)

    @pl.when(j == 0)
    def _():
        x = x_ref[...]
        h_ref[...] = _rms(x, g_ref[...]).astype(BF16)
        o_ref[...] = x

    h = h_ref[...]
    a = _dot(h, wg_ref[...])
    b = _dot(h, wu_ref[...])
    mid = (a * jax.nn.sigmoid(a) * b).astype(BF16)
    o_ref[...] += _dot(mid, wd_ref[...])

    if final_norm:
        @pl.when(j == pl.num_programs(1) - 1)
        def _():
            o_ref[...] = _rms(o_ref[...], gf_ref[...])


def _ffn_call(x2, g, wg, wu, wd, g_final, final_norm, tm=1024, tf=512):
    t = x2.shape[0]
    d_ff = wg.shape[1]
    return pl.pallas_call(
        functools.partial(_ffn_kernel, final_norm=final_norm),
        grid=(t // tm, d_ff // tf),
        in_specs=[
            pl.BlockSpec((tm, D_MODEL), lambda i, j: (i, 0)),
            _resident((1, D_MODEL)),
            pl.BlockSpec((D_MODEL, tf), lambda i, j: (0, j)),
            pl.BlockSpec((D_MODEL, tf), lambda i, j: (0, j)),
            pl.BlockSpec((tf, D_MODEL), lambda i, j: (j, 0)),
            _resident((1, D_MODEL)),
        ],
        out_specs=pl.BlockSpec((tm, D_MODEL), lambda i, j: (i, 0)),
        out_shape=jax.ShapeDtypeStruct((t, D_MODEL), F32),
        scratch_shapes=[pltpu.VMEM((tm, D_MODEL), BF16)],
        compiler_params=_params(60, ("parallel", "arbitrary")),
        name="swiglu_ffn",
    )(x2, g, wg, wu, wd, g_final)


def _rope_tables(seq):
    pos = jnp.arange(seq, dtype=jnp.int32)
    row = (pos // GRID_W).astype(F32)
    col = (pos % GRID_W).astype(F32)
    inv = ROPE_THETA ** (-jnp.arange(ROPE_FREQ_PER_AXIS, dtype=F32) / ROPE_FREQ_PER_AXIS)
    ang = jnp.concatenate([row[:, None] * inv, col[:, None] * inv], axis=-1)
    cos, sin = jnp.cos(ang), jnp.sin(ang)
    return jnp.concatenate([cos, cos], axis=-1), jnp.concatenate([-sin, sin], axis=-1)


def kernel(x, g_mix, w_in, b_gate, q_norm_g, k_norm_g, w_attn_o, w_dw, b_dw, conv_ln_g, conv_ln_b, w_conv_o, sg_ln_g, sg_ln_b, w_s, b_s, w_sg_o, w_out, g_ffn, w_ff_gate, w_ff_up, w_ff_down, g_final):
    batch, seq, _ = x.shape
    depth = w_in.shape[0]
    cos2, sin2 = _rope_tables(seq)
    x2 = x.reshape(batch * seq, D_MODEL)
    row = lambda p: p.reshape(1, -1)
    c0 = QKV_COLS
    c1 = c0 + CONV_CH
    c2 = c1 + CONV_CH
    c3 = c2 + SG_CH
    c4 = c3 + SG_CH

    for l in range(depth):
        w = w_in[l]
        h, qkv = _qkv_call(x2, row(g_mix[l]), w[:, :c0].astype(BF16), row(q_norm_g[l]),
                           row(k_norm_g[l]), cos2, sin2, seq)
        z = _glu_call(h, w[:, c0:c1].astype(BF16), w[:, c1:c2].astype(BF16))
        u, v = _sgu_in_call(h, w[:, c2:c3].astype(BF16), w[:, c3:c4].astype(BF16),
                            row(sg_ln_g[l]), row(sg_ln_b[l]))
        gates = _gates_call(h, w[:, c4:].astype(BF16), row(b_gate[l]))
        attn = _attn_call(qkv, batch, seq)
        conv = _conv_call(z, w_dw[l].reshape(CONV_WIDTH, CONV_CH), row(b_dw[l]),
                          row(conv_ln_g[l]), row(conv_ln_b[l]), batch, seq)
        sgu = _sgu_mix_call(u, v, w_s[l].astype(BF16), b_s[l].T)
        x2 = _merge_call(attn, conv, sgu, gates, x2, w_attn_o[l].astype(BF16),
                         w_conv_o[l].astype(BF16), w_sg_o[l].astype(BF16), w_out[l].astype(BF16))
        x2 = _ffn_call(x2, row(g_ffn[l]), w_ff_gate[l].astype(BF16), w_ff_up[l].astype(BF16),
                       w_ff_down[l].astype(BF16), row(g_final), final_norm=(l == depth - 1))
    return x2.reshape(batch, seq, D_MODEL)
```

```python
import functools

import jax
import jax.numpy as jnp
from jax import lax
from jax.experimental import pallas as pl
from jax.experimental.pallas import tpu as pltpu

D_MODEL = 2048
GRID_W = 64
HEAD_DIM = 128
N_Q_HEADS = 8
N_KV_HEADS = 2
Q_PER_KV = N_Q_HEADS // N_KV_HEADS
ROPE_THETA = 10000.0
ROPE_PAIRS = HEAD_DIM // 2
ROPE_FREQ_PER_AXIS = ROPE_PAIRS // 2
CONV_CH = 1024
CONV_WIDTH = 31
CONV_PAD = CONV_WIDTH // 2
SG_CH = 1024
SG_GROUP_CH = 128
SG_GROUPS = SG_CH // SG_GROUP_CH
SG_CHUNK = 128
N_BRANCH = 3
Q_COLS = N_Q_HEADS * HEAD_DIM
KV_COLS = N_KV_HEADS * HEAD_DIM
QKV_COLS = Q_COLS + 2 * KV_COLS
RMS_EPS = 1e-6
LN_EPS = 1e-5
LOG2_E = 1.4426950408889634

V7X_VMEM_BYTES = 64 * 1024 * 1024
V7X_SUBLANES = 8
V7X_LANES = 128
CONV_HALO = -(-CONV_PAD // V7X_SUBLANES) * V7X_SUBLANES

BF16 = jnp.bfloat16
F32 = jnp.float32


def _params(vmem_mib, semantics):
    return pltpu.CompilerParams(
        dimension_semantics=semantics, vmem_limit_bytes=vmem_mib * 1024 * 1024)


def _resident(shape):
    return pl.BlockSpec(shape, lambda *_: (0,) * len(shape), pipeline_mode=pl.Buffered(1))


def _rms(xf, gain):
    return xf * lax.rsqrt(jnp.mean(xf * xf, axis=-1, keepdims=True) + RMS_EPS) * gain


def _layer_norm(xf, gain, bias):
    mu = jnp.mean(xf, axis=-1, keepdims=True)
    xc = xf - mu
    return xc * lax.rsqrt(jnp.mean(xc * xc, axis=-1, keepdims=True) + LN_EPS) * gain + bias


def _dot(a, b):
    return jnp.dot(a, b, preferred_element_type=F32)


CAST_COL_BLOCK = 512
CAST_MAX_ROW_BLOCK = 2048


def _cast_kernel(w_ref, o_ref):
    o_ref[...] = w_ref[...].astype(BF16)


def _cast_call(w, layer, col0=0, ncols=None):
    _, rows, cols = w.shape
    ncols = cols - col0 if ncols is None else ncols
    assert col0 % CAST_COL_BLOCK == 0 and ncols % CAST_COL_BLOCK == 0
    row_blk = next(r for r in range(min(rows, CAST_MAX_ROW_BLOCK), 0, -V7X_SUBLANES) if rows % r == 0)
    cb0 = col0 // CAST_COL_BLOCK
    return pl.pallas_call(
        _cast_kernel,
        grid=(rows // row_blk, ncols // CAST_COL_BLOCK),
        in_specs=[pl.BlockSpec((None, row_blk, CAST_COL_BLOCK), lambda i, j: (layer, i, cb0 + j))],
        out_specs=pl.BlockSpec((row_blk, CAST_COL_BLOCK), lambda i, j: (i, j)),
        out_shape=jax.ShapeDtypeStruct((rows, ncols), BF16),
        compiler_params=_params(32, ("parallel", "parallel")),
        name="weight_to_bf16",
    )(w)


def _qkv_kernel(x_ref, g_ref, w_ref, qg_ref, kg_ref, cos_ref, sin_ref, h_ref, qkv_ref):
    h = _rms(x_ref[...], g_ref[...]).astype(BF16)
    h_ref[...] = h
    acc = _dot(h, w_ref[...])
    cos = cos_ref[...]
    sin = sin_ref[...]
    scale = HEAD_DIM ** -0.5 * LOG2_E
    for head in range(N_Q_HEADS + N_KV_HEADS):
        lo = head * HEAD_DIM
        is_q = head < N_Q_HEADS
        gain = qg_ref[...] if is_q else kg_ref[...]
        y = _rms(acc[:, lo:lo + HEAD_DIM], gain)
        y = y * cos + pltpu.roll(y, ROPE_PAIRS, axis=1) * sin
        if is_q:
            y = y * scale
        qkv_ref[:, lo:lo + HEAD_DIM] = y.astype(BF16)
    v_lo = Q_COLS + KV_COLS
    qkv_ref[:, v_lo:] = acc[:, v_lo:].astype(BF16)


def _qkv_call(x2, g, w, qg, kg, cos2, sin2, seq, tm=512):
    t = x2.shape[0]
    pos_blocks = seq // tm
    return pl.pallas_call(
        _qkv_kernel,
        grid=(t // tm,),
        in_specs=[
            pl.BlockSpec((tm, D_MODEL), lambda i: (i, 0)),
            _resident((1, D_MODEL)),
            _resident((D_MODEL, QKV_COLS)),
            _resident((1, HEAD_DIM)),
            _resident((1, HEAD_DIM)),
            pl.BlockSpec((tm, HEAD_DIM), lambda i: (i % pos_blocks, 0)),
            pl.BlockSpec((tm, HEAD_DIM), lambda i: (i % pos_blocks, 0)),
        ],
        out_specs=[
            pl.BlockSpec((tm, D_MODEL), lambda i: (i, 0)),
            pl.BlockSpec((tm, QKV_COLS), lambda i: (i, 0)),
        ],
        out_shape=[
            jax.ShapeDtypeStruct((t, D_MODEL), BF16),
            jax.ShapeDtypeStruct((t, QKV_COLS), BF16),
        ],
        compiler_params=_params(48, ("parallel",)),
        name="rms_qkv_rope",
    )(x2, g, w, qg, kg, cos2, sin2)


def _glu_kernel(h_ref, wa_ref, wg_ref, z_ref):
    h = h_ref[...]
    a = _dot(h, wa_ref[...])
    g = _dot(h, wg_ref[...])
    z_ref[...] = (a * jax.nn.sigmoid(g)).astype(BF16)


def _glu_call(h, wa, wg, tm=1024):
    t = h.shape[0]
    return pl.pallas_call(
        _glu_kernel,
        grid=(t // tm,),
        in_specs=[
            pl.BlockSpec((tm, D_MODEL), lambda i: (i, 0)),
            _resident((D_MODEL, CONV_CH)),
            _resident((D_MODEL, CONV_CH)),
        ],
        out_specs=pl.BlockSpec((tm, CONV_CH), lambda i: (i, 0)),
        out_shape=jax.ShapeDtypeStruct((t, CONV_CH), BF16),
        compiler_params=_params(48, ("parallel",)),
        name="conv_glu_in",
    )(h, wa, wg)


def _sgu_in_kernel(h_ref, wu_ref, wv_ref, g_ref, b_ref, u_ref, v_ref):
    h = h_ref[...]
    u_ref[...] = jax.nn.gelu(_dot(h, wu_ref[...])).astype(BF16)
    v = jax.nn.gelu(_dot(h, wv_ref[...]))
    for grp in range(SG_GROUPS):
        lo = grp * SG_GROUP_CH
        hi = lo + SG_GROUP_CH
        v_ref[:, lo:hi] = _layer_norm(v[:, lo:hi], g_ref[:, lo:hi], b_ref[:, lo:hi]).astype(BF16)


def _sgu_in_call(h, wu, wv, ln_g, ln_b, tm=1024):
    t = h.shape[0]
    return pl.pallas_call(
        _sgu_in_kernel,
        grid=(t // tm,),
        in_specs=[
            pl.BlockSpec((tm, D_MODEL), lambda i: (i, 0)),
            _resident((D_MODEL, SG_CH)),
            _resident((D_MODEL, SG_CH)),
            _resident((1, SG_CH)),
            _resident((1, SG_CH)),
        ],
        out_specs=[
            pl.BlockSpec((tm, SG_CH), lambda i: (i, 0)),
            pl.BlockSpec((tm, SG_CH), lambda i: (i, 0)),
        ],
        out_shape=[
            jax.ShapeDtypeStruct((t, SG_CH), BF16),
            jax.ShapeDtypeStruct((t, SG_CH), BF16),
        ],
        compiler_params=_params(48, ("parallel",)),
        name="sgu_in",
    )(h, wu, wv, ln_g, ln_b)


def _gates_kernel(h_ref, w_ref, b_ref, o_ref):
    o_ref[...] = jax.nn.sigmoid(_dot(h_ref[...], w_ref[...]) + b_ref[...]).astype(BF16)


def _gates_call(h, w, b, tm=1024, tn=1024):
    t = h.shape[0]
    n = w.shape[1]
    return pl.pallas_call(
        _gates_kernel,
        grid=(t // tm, n // tn),
        in_specs=[
            pl.BlockSpec((tm, D_MODEL), lambda i, j: (i, 0)),
            pl.BlockSpec((D_MODEL, tn), lambda i, j: (0, j)),
            pl.BlockSpec((1, tn), lambda i, j: (0, j)),
        ],
        out_specs=pl.BlockSpec((tm, tn), lambda i, j: (i, j)),
        out_shape=jax.ShapeDtypeStruct((t, n), BF16),
        compiler_params=_params(48, ("parallel", "arbitrary")),
        name="branch_gates",
    )(h, w, b)


def _attn_kernel(q_ref, k_ref, v_ref, o_ref, *, sub_rows):
    k = k_ref[...]
    v = v_ref[...]
    v_ext = jnp.concatenate([v, jnp.ones_like(v)], axis=1)
    tq = q_ref.shape[0]
    for r0 in range(0, tq, sub_rows):
        for grp in range(Q_PER_KV):
            lo = grp * HEAD_DIM
            q = q_ref[r0:r0 + sub_rows, lo:lo + HEAD_DIM]
            s = lax.dot_general(q, k, (((1,), (1,)), ((), ())), preferred_element_type=F32)
            p = jnp.exp2(s - jnp.max(s, axis=-1, keepdims=True)).astype(BF16)
            o = _dot(p, v_ext)
            o_ref[r0:r0 + sub_rows, lo:lo + HEAD_DIM] = (o[:, :HEAD_DIM] / o[:, HEAD_DIM:]).astype(BF16)


def _attn_call(qkv, batch, seq, tq=1024):
    t = qkv.shape[0]
    q_blocks = seq // tq
    grp_cols = Q_PER_KV * HEAD_DIM
    k_col0 = Q_COLS // HEAD_DIM
    v_col0 = (Q_COLS + KV_COLS) // HEAD_DIM
    return pl.pallas_call(
        functools.partial(_attn_kernel, sub_rows=256),
        grid=(batch, N_KV_HEADS, q_blocks),
        in_specs=[
            pl.BlockSpec((tq, grp_cols), lambda b, h, i: (b * q_blocks + i, h)),
            pl.BlockSpec((seq, HEAD_DIM), lambda b, h, i: (b, k_col0 + h)),
            pl.BlockSpec((seq, HEAD_DIM), lambda b, h, i: (b, v_col0 + h)),
        ],
        out_specs=pl.BlockSpec((tq, grp_cols), lambda b, h, i: (b * q_blocks + i, h)),
        out_shape=jax.ShapeDtypeStruct((t, Q_COLS), BF16),
        compiler_params=_params(48, ("parallel", "parallel", "arbitrary")),
        name="gqa_attention",
    )(qkv, qkv, qkv)


def _conv_kernel(z_ref, w_ref, b_ref, g_ref, beta_ref, o_ref, pad_ref, conv_ref, *, seq, rows):
    zeros = jnp.zeros((CONV_HALO, CONV_CH), F32)
    pad_ref[0:CONV_HALO, :] = zeros
    pad_ref[CONV_HALO + seq:, :] = zeros
    pad_ref[CONV_HALO:CONV_HALO + seq, :] = z_ref[...].astype(F32)
    win_rows = rows + 2 * CONV_HALO
    lane_blocks = CONV_CH // V7X_LANES
    row_groups = 2 * CONV_HALO // V7X_SUBLANES

    def tile(i, carry):
        r0 = pl.multiple_of(i * rows, rows)
        for cb in range(lane_blocks):
            lanes = slice(cb * V7X_LANES, (cb + 1) * V7X_LANES)
            win = pad_ref[pl.ds(r0, win_rows), lanes]
            acc = jnp.zeros((rows, V7X_LANES), F32)
            for sub in range(V7X_SUBLANES):
                shifted = win if sub == 0 else pltpu.roll(win, win_rows - sub, axis=0)
                for grp in range(row_groups):
                    tap = V7X_SUBLANES * grp + sub - (CONV_HALO - CONV_PAD)
                    if 0 <= tap < CONV_WIDTH:
                        lo = V7X_SUBLANES * grp
                        acc = acc + shifted[lo:lo + rows, :] * w_ref[tap:tap + 1, lanes]
            conv_ref[:, lanes] = acc + b_ref[:, lanes]
        y = _layer_norm(conv_ref[...], g_ref[...], beta_ref[...])
        o_ref[pl.ds(r0, rows), :] = (y * jax.nn.sigmoid(y)).astype(BF16)
        return carry

    lax.fori_loop(0, seq // rows, tile, 0)


def _conv_call(z, w_dw, b_dw, ln_g, ln_b, batch, seq, rows=64):
    t = z.shape[0]
    return pl.pallas_call(
        functools.partial(_conv_kernel, seq=seq, rows=rows),
        grid=(batch,),
        in_specs=[
            pl.BlockSpec((seq, CONV_CH), lambda b: (b, 0)),
            _resident((CONV_WIDTH, CONV_CH)),
            _resident((1, CONV_CH)),
            _resident((1, CONV_CH)),
            _resident((1, CONV_CH)),
        ],
        out_specs=pl.BlockSpec((seq, CONV_CH), lambda b: (b, 0)),
        out_shape=jax.ShapeDtypeStruct((t, CONV_CH), BF16),
        scratch_shapes=[pltpu.VMEM((seq + 2 * CONV_HALO, CONV_CH), F32),
                        pltpu.VMEM((rows, CONV_CH), F32)],
        compiler_params=_params(48, ("parallel",)),
        name="conv_ln_swish",
    )(z, w_dw, b_dw, ln_g, ln_b)


def _sgu_mix_kernel(u_ref, v_ref, ws_ref, bst_ref, o_ref, *, chunks):
    for grp in range(SG_GROUPS):
        lo = grp * SG_GROUP_CH
        w = ws_ref[grp].astype(BF16)
        bias = bst_ref[:, grp:grp + 1]
        for c in range(chunks):
            r0 = c * SG_CHUNK
            mixed = _dot(w, v_ref[r0:r0 + SG_CHUNK, lo:lo + SG_GROUP_CH]) + bias
            u = u_ref[r0:r0 + SG_CHUNK, lo:lo + SG_GROUP_CH].astype(F32)
            o_ref[r0:r0 + SG_CHUNK, lo:lo + SG_GROUP_CH] = (u * mixed).astype(BF16)


def _sgu_mix_call(u, v, ws, bst, tm=1024):
    t = u.shape[0]
    return pl.pallas_call(
        functools.partial(_sgu_mix_kernel, chunks=tm // SG_CHUNK),
        grid=(t // tm,),
        in_specs=[
            pl.BlockSpec((tm, SG_CH), lambda i: (i, 0)),
            pl.BlockSpec((tm, SG_CH), lambda i: (i, 0)),
            _resident((SG_GROUPS, SG_CHUNK, SG_CHUNK)),
            _resident((SG_CHUNK, SG_GROUPS)),
        ],
        out_specs=pl.BlockSpec((tm, SG_CH), lambda i: (i, 0)),
        out_shape=jax.ShapeDtypeStruct((t, SG_CH), BF16),
        compiler_params=_params(48, ("parallel",)),
        name="sgu_mix",
    )(u, v, ws, bst)


def _merge_kernel(a_ref, c_ref, s_ref, gate_ref, x_ref, wa_ref, wc_ref, ws_ref, wo_ref, o_ref):
    merged = gate_ref[:, 0:D_MODEL].astype(F32) * _dot(a_ref[...], wa_ref[...])
    merged = merged + gate_ref[:, D_MODEL:2 * D_MODEL].astype(F32) * _dot(c_ref[...], wc_ref[...])
    merged = merged + gate_ref[:, 2 * D_MODEL:].astype(F32) * _dot(s_ref[...], ws_ref[...])
    o_ref[...] = x_ref[...] + _dot(merged.astype(BF16), wo_ref[...])


def _merge_call(attn, conv, sgu, gates, x2, wa, wc, ws, wo, tm=256):
    t = x2.shape[0]
    half = D_MODEL // 2
    return pl.pallas_call(
        _merge_kernel,
        grid=(t // tm,),
        in_specs=[
            pl.BlockSpec((tm, half), lambda i: (i, 0)),
            pl.BlockSpec((tm, half), lambda i: (i, 0)),
            pl.BlockSpec((tm, half), lambda i: (i, 0)),
            pl.BlockSpec((tm, N_BRANCH * D_MODEL), lambda i: (i, 0)),
            pl.BlockSpec((tm, D_MODEL), lambda i: (i, 0)),
            _resident((half, D_MODEL)),
            _resident((half, D_MODEL)),
            _resident((half, D_MODEL)),
            _resident((D_MODEL, D_MODEL)),
        ],
        out_specs=pl.BlockSpec((tm, D_MODEL), lambda i: (i, 0)),
        out_shape=jax.ShapeDtypeStruct((t, D_MODEL), F32),
        compiler_params=_params(56, ("parallel",)),
        name="merge_out_proj",
    )(attn, conv, sgu, gates, x2, wa, wc, ws, wo)


def _ffn_kernel(x_ref, g_ref, wg_ref, wu_ref, wd_ref, gf_ref, o_ref, h_ref, *, final_norm):
    j = pl.program_id(1)

    @pl.when(j == 0)
    def _():
        x = x_ref[...]
        h_ref[...] = _rms(x, g_ref[...]).astype(BF16)
        o_ref[...] = x

    h = h_ref[...]
    a = _dot(h, wg_ref[...])
    b = _dot(h, wu_ref[...])
    mid = (a * jax.nn.sigmoid(a) * b).astype(BF16)
    o_ref[...] += _dot(mid, wd_ref[...])

    if final_norm:
        @pl.when(j == pl.num_programs(1) - 1)
        def _():
            o_ref[...] = _rms(o_ref[...], gf_ref[...])


def _ffn_call(x2, g, wg, wu, wd, g_final, final_norm, tm=1024, tf=512):
    t = x2.shape[0]
    d_ff = wg.shape[1]
    return pl.pallas_call(
        functools.partial(_ffn_kernel, final_norm=final_norm),
        grid=(t // tm, d_ff // tf),
        in_specs=[
            pl.BlockSpec((tm, D_MODEL), lambda i, j: (i, 0)),
            _resident((1, D_MODEL)),
            pl.BlockSpec((D_MODEL, tf), lambda i, j: (0, j)),
            pl.BlockSpec((D_MODEL, tf), lambda i, j: (0, j)),
            pl.BlockSpec((tf, D_MODEL), lambda i, j: (j, 0)),
            _resident((1, D_MODEL)),
        ],
        out_specs=pl.BlockSpec((tm, D_MODEL), lambda i, j: (i, 0)),
        out_shape=jax.ShapeDtypeStruct((t, D_MODEL), F32),
        scratch_shapes=[pltpu.VMEM((tm, D_MODEL), BF16)],
        compiler_params=_params(60, ("parallel", "arbitrary")),
        name="swiglu_ffn",
    )(x2, g, wg, wu, wd, g_final)


def _rope_tables(seq):
    pos = jnp.arange(seq, dtype=jnp.int32)
    row = (pos // GRID_W).astype(F32)
    col = (pos % GRID_W).astype(F32)
    inv = ROPE_THETA ** (-jnp.arange(ROPE_FREQ_PER_AXIS, dtype=F32) / ROPE_FREQ_PER_AXIS)
    ang = jnp.concatenate([row[:, None] * inv, col[:, None] * inv], axis=-1)
    cos, sin = jnp.cos(ang), jnp.sin(ang)
    return jnp.concatenate([cos, cos], axis=-1), jnp.concatenate([-sin, sin], axis=-1)


def kernel(x, g_mix, w_in, b_gate, q_norm_g, k_norm_g, w_attn_o, w_dw, b_dw, conv_ln_g, conv_ln_b, w_conv_o, sg_ln_g, sg_ln_b, w_s, b_s, w_sg_o, w_out, g_ffn, w_ff_gate, w_ff_up, w_ff_down, g_final):
    batch, seq, _ = x.shape
    depth = w_in.shape[0]
    cos2, sin2 = _rope_tables(seq)
    x2 = x.reshape(batch * seq, D_MODEL)
    row = lambda p: p.reshape(1, -1)
    c0 = QKV_COLS
    c1 = c0 + CONV_CH
    c2 = c1 + CONV_CH
    c3 = c2 + SG_CH
    c4 = c3 + SG_CH

    for l in range(depth):
        h, qkv = _qkv_call(x2, row(g_mix[l]), _cast_call(w_in, l, 0, c0), row(q_norm_g[l]),
                           row(k_norm_g[l]), cos2, sin2, seq)
        z = _glu_call(h, _cast_call(w_in, l, c0, CONV_CH), _cast_call(w_in, l, c1, CONV_CH))
        u, v = _sgu_in_call(h, _cast_call(w_in, l, c2, SG_CH), _cast_call(w_in, l, c3, SG_CH),
                            row(sg_ln_g[l]), row(sg_ln_b[l]))
        gates = _gates_call(h, _cast_call(w_in, l, c4), row(b_gate[l]))
        attn = _attn_call(qkv, batch, seq)
        conv = _conv_call(z, w_dw[l].reshape(CONV_WIDTH, CONV_CH), row(b_dw[l]),
                          row(conv_ln_g[l]), row(conv_ln_b[l]), batch, seq)
        sgu = _sgu_mix_call(u, v, w_s[l], b_s[l].T)
        x2 = _merge_call(attn, conv, sgu, gates, x2, _cast_call(w_attn_o, l), _cast_call(w_conv_o, l),
                         _cast_call(w_sg_o, l), _cast_call(w_out, l))
        x2 = _ffn_call(x2, row(g_ffn[l]), _cast_call(w_ff_gate, l), _cast_call(w_ff_up, l),
                       _cast_call(w_ff_down, l), row(g_final), final_norm=(l == depth - 1))
    return x2.reshape(batch, seq, D_MODEL)
```

```python
import functools

import jax
import jax.numpy as jnp
from jax import lax
from jax.experimental import pallas as pl
from jax.experimental.pallas import tpu as pltpu

D_MODEL = 2048
GRID_W = 64
HEAD_DIM = 128
N_Q_HEADS = 8
N_KV_HEADS = 2
Q_PER_KV = N_Q_HEADS // N_KV_HEADS
ROPE_THETA = 10000.0
ROPE_PAIRS = HEAD_DIM // 2
ROPE_FREQ_PER_AXIS = ROPE_PAIRS // 2
CONV_CH = 1024
CONV_WIDTH = 31
CONV_PAD = CONV_WIDTH // 2
SG_CH = 1024
SG_GROUP_CH = 128
SG_GROUPS = SG_CH // SG_GROUP_CH
SG_CHUNK = 128
N_BRANCH = 3
Q_COLS = N_Q_HEADS * HEAD_DIM
KV_COLS = N_KV_HEADS * HEAD_DIM
QKV_COLS = Q_COLS + 2 * KV_COLS
RMS_EPS = 1e-6
LN_EPS = 1e-5
LOG2_E = 1.4426950408889634

V7X_VMEM_BYTES = 64 * 1024 * 1024
V7X_SUBLANES = 8
V7X_LANES = 128
CONV_HALO = -(-CONV_PAD // V7X_SUBLANES) * V7X_SUBLANES

BF16 = jnp.bfloat16
F32 = jnp.float32


def _params(semantics):
    return pltpu.CompilerParams(dimension_semantics=semantics, vmem_limit_bytes=V7X_VMEM_BYTES)


def _resident(shape):
    return pl.BlockSpec(shape, lambda *_: (0,) * len(shape), pipeline_mode=pl.Buffered(1))


def _rms(xf, gain):
    return xf * lax.rsqrt(jnp.mean(xf * xf, axis=-1, keepdims=True) + RMS_EPS) * gain


def _layer_norm(xf, gain, bias):
    mu = jnp.mean(xf, axis=-1, keepdims=True)
    xc = xf - mu
    return xc * lax.rsqrt(jnp.mean(xc * xc, axis=-1, keepdims=True) + LN_EPS) * gain + bias


def _dot(a, b):
    return jnp.dot(a, b, preferred_element_type=F32)


CAST_COL_BLOCK = 512
CAST_MAX_ROW_BLOCK = 2048


def _cast_kernel(w_ref, o_ref):
    o_ref[...] = w_ref[...].astype(BF16)


def _cast_call(w, layer, col0=0, ncols=None):
    _, rows, cols = w.shape
    ncols = cols - col0 if ncols is None else ncols
    assert col0 % CAST_COL_BLOCK == 0 and ncols % CAST_COL_BLOCK == 0
    row_blk = next(r for r in range(min(rows, CAST_MAX_ROW_BLOCK), 0, -V7X_SUBLANES) if rows % r == 0)
    cb0 = col0 // CAST_COL_BLOCK
    return pl.pallas_call(
        _cast_kernel,
        grid=(rows // row_blk, ncols // CAST_COL_BLOCK),
        in_specs=[pl.BlockSpec((None, row_blk, CAST_COL_BLOCK), lambda i, j: (layer, i, cb0 + j))],
        out_specs=pl.BlockSpec((row_blk, CAST_COL_BLOCK), lambda i, j: (i, j)),
        out_shape=jax.ShapeDtypeStruct((rows, ncols), BF16),
        compiler_params=_params(("parallel", "parallel")),
        name="weight_to_bf16",
    )(w)


def _qkv_kernel(x_ref, g_ref, w_ref, h_ref, qkv_ref):
    h = _rms(x_ref[...], g_ref[...]).astype(BF16)
    h_ref[...] = h
    qkv_ref[...] = _dot(h, w_ref[...]).astype(BF16)


def _qkv_call(x2, g, w, tm=512):
    t = x2.shape[0]
    return pl.pallas_call(
        _qkv_kernel,
        grid=(t // tm,),
        in_specs=[
            pl.BlockSpec((tm, D_MODEL), lambda i: (i, 0)),
            _resident((1, D_MODEL)),
            _resident((D_MODEL, QKV_COLS)),
        ],
        out_specs=[
            pl.BlockSpec((tm, D_MODEL), lambda i: (i, 0)),
            pl.BlockSpec((tm, QKV_COLS), lambda i: (i, 0)),
        ],
        out_shape=[
            jax.ShapeDtypeStruct((t, D_MODEL), BF16),
            jax.ShapeDtypeStruct((t, QKV_COLS), BF16),
        ],
        compiler_params=_params(("parallel",)),
        name="rms_qkv",
    )(x2, g, w)


def _glu_kernel(h_ref, wa_ref, wg_ref, z_ref):
    h = h_ref[...]
    a = _dot(h, wa_ref[...])
    g = _dot(h, wg_ref[...])
    z_ref[...] = (a * jax.nn.sigmoid(g)).astype(BF16)


def _glu_call(h, wa, wg, tm=1024):
    t = h.shape[0]
    return pl.pallas_call(
        _glu_kernel,
        grid=(t // tm,),
        in_specs=[
            pl.BlockSpec((tm, D_MODEL), lambda i: (i, 0)),
            _resident((D_MODEL, CONV_CH)),
            _resident((D_MODEL, CONV_CH)),
        ],
        out_specs=pl.BlockSpec((tm, CONV_CH), lambda i: (i, 0)),
        out_shape=jax.ShapeDtypeStruct((t, CONV_CH), BF16),
        compiler_params=_params(("parallel",)),
        name="conv_glu_in",
    )(h, wa, wg)


def _sgu_kernel(h_ref, wu_ref, wv_ref, g_ref, b_ref, ws_ref, bst_ref, o_ref):
    h = h_ref[...]
    u = jax.nn.gelu(_dot(h, wu_ref[...]))
    v = jax.nn.gelu(_dot(h, wv_ref[...]))
    for grp in range(SG_GROUPS):
        cols = slice(grp * SG_GROUP_CH, (grp + 1) * SG_GROUP_CH)
        vn = _layer_norm(v[:, cols], g_ref[:, cols], b_ref[:, cols]).astype(BF16)
        w = ws_ref[grp].astype(BF16)
        bias = bst_ref[:, grp:grp + 1]
        for r0 in range(0, h.shape[0], SG_CHUNK):
            rows = slice(r0, r0 + SG_CHUNK)
            o_ref[rows, cols] = (u[rows, cols] * (_dot(w, vn[rows, :]) + bias)).astype(BF16)


def _sgu_call(h, wu, wv, ln_g, ln_b, ws, bst, tm=1024):
    t = h.shape[0]
    assert tm % SG_CHUNK == 0
    return pl.pallas_call(
        _sgu_kernel,
        grid=(t // tm,),
        in_specs=[
            pl.BlockSpec((tm, D_MODEL), lambda i: (i, 0)),
            _resident((D_MODEL, SG_CH)),
            _resident((D_MODEL, SG_CH)),
            _resident((1, SG_CH)),
            _resident((1, SG_CH)),
            _resident((SG_GROUPS, SG_CHUNK, SG_CHUNK)),
            _resident((SG_CHUNK, SG_GROUPS)),
        ],
        out_specs=pl.BlockSpec((tm, SG_CH), lambda i: (i, 0)),
        out_shape=jax.ShapeDtypeStruct((t, SG_CH), BF16),
        compiler_params=_params(("parallel",)),
        name="spatial_gating",
    )(h, wu, wv, ln_g, ln_b, ws, bst)


GATE_COL_CHUNK = 1024


def _gates_kernel(h_ref, w_ref, b_ref, o_ref):
    h = h_ref[...]
    for c0 in range(0, o_ref.shape[1], GATE_COL_CHUNK):
        cols = slice(c0, c0 + GATE_COL_CHUNK)
        o_ref[:, cols] = jax.nn.sigmoid(_dot(h, w_ref[:, cols]) + b_ref[:, cols]).astype(BF16)


def _gates_call(h, w, b, tm=512):
    t = h.shape[0]
    n = w.shape[1]
    return pl.pallas_call(
        _gates_kernel,
        grid=(t // tm,),
        in_specs=[
            pl.BlockSpec((tm, D_MODEL), lambda i: (i, 0)),
            _resident((D_MODEL, n)),
            _resident((1, n)),
        ],
        out_specs=pl.BlockSpec((tm, n), lambda i: (i, 0)),
        out_shape=jax.ShapeDtypeStruct((t, n), BF16),
        compiler_params=_params(("parallel",)),
        name="branch_gates",
    )(h, w, b)


def _norm_rope(y, cos, sin):
    y = y.astype(F32)
    rinv = lax.rsqrt(jnp.mean(y * y, axis=-1, keepdims=True) + RMS_EPS)
    return ((y * cos + pltpu.roll(y, ROPE_PAIRS, axis=1) * sin) * rinv).astype(BF16)


def _attn_kernel(q_ref, k_ref, v_ref, qcos_ref, qsin_ref, kcos_ref, ksin_ref, o_ref, k_scr, *, sub_rows):
    @pl.when(pl.program_id(2) == 0)
    def _():
        k_scr[...] = _norm_rope(k_ref[...], kcos_ref[...], ksin_ref[...])

    k = k_scr[...]
    v = v_ref[...]
    v_ext = jnp.concatenate([v, jnp.ones_like(v)], axis=1)
    tq = q_ref.shape[0]
    for r0 in range(0, tq, sub_rows):
        rows = slice(r0, r0 + sub_rows)
        for grp in range(Q_PER_KV):
            lo = grp * HEAD_DIM
            q = _norm_rope(q_ref[rows, lo:lo + HEAD_DIM], qcos_ref[rows, :], qsin_ref[rows, :])
            s = lax.dot_general(q, k, (((1,), (1,)), ((), ())), preferred_element_type=F32)
            p = jnp.exp2(s - jnp.max(s, axis=-1, keepdims=True)).astype(BF16)
            o = _dot(p, v_ext)
            o_ref[rows, lo:lo + HEAD_DIM] = (o[:, :HEAD_DIM] / o[:, HEAD_DIM:]).astype(BF16)


def _attn_call(qkv, tables, batch, seq, tq=1024):
    t = qkv.shape[0]
    q_blocks = seq // tq
    grp_cols = Q_PER_KV * HEAD_DIM
    k_col0 = Q_COLS // HEAD_DIM
    v_col0 = (Q_COLS + KV_COLS) // HEAD_DIM
    q_table = pl.BlockSpec((tq, HEAD_DIM), lambda b, h, i: (i, 0))
    k_table = pl.BlockSpec((seq, HEAD_DIM), lambda b, h, i: (0, 0))
    return pl.pallas_call(
        functools.partial(_attn_kernel, sub_rows=256),
        grid=(batch, N_KV_HEADS, q_blocks),
        in_specs=[
            pl.BlockSpec((tq, grp_cols), lambda b, h, i: (b * q_blocks + i, h)),
            pl.BlockSpec((seq, HEAD_DIM), lambda b, h, i: (b, k_col0 + h)),
            pl.BlockSpec((seq, HEAD_DIM), lambda b, h, i: (b, v_col0 + h)),
            q_table, q_table, k_table, k_table,
        ],
        out_specs=pl.BlockSpec((tq, grp_cols), lambda b, h, i: (b * q_blocks + i, h)),
        out_shape=jax.ShapeDtypeStruct((t, Q_COLS), BF16),
        scratch_shapes=[pltpu.VMEM((seq, HEAD_DIM), BF16)],
        compiler_params=_params(("parallel", "parallel", "arbitrary")),
        name="gqa_attention",
    )(qkv, qkv, qkv, *tables)


def _conv_kernel(z_ref, w_ref, b_ref, g_ref, beta_ref, o_ref, pad_ref, conv_ref, *, seq, rows,
                 tiles_per_step):
    zeros = jnp.zeros((CONV_HALO, CONV_CH), BF16)
    pad_ref[0:CONV_HALO, :] = zeros
    pad_ref[CONV_HALO + seq:, :] = zeros
    pad_ref[CONV_HALO:CONV_HALO + seq, :] = z_ref[...]
    win_rows = rows + 2 * CONV_HALO
    lane_blocks = CONV_CH // V7X_LANES
    row_groups = 2 * CONV_HALO // V7X_SUBLANES
    n_shifts = V7X_SUBLANES - 1
    r = lax.broadcasted_iota(jnp.int32, (n_shifts * win_rows, win_rows), 0)
    c = lax.broadcasted_iota(jnp.int32, (n_shifts * win_rows, win_rows), 1)
    shift_mat = (c == r % win_rows + r // win_rows + 1).astype(BF16)

    def conv_tile(r0, tile_ref):
        win = pad_ref[pl.ds(r0, win_rows), :]
        shifted_all = _dot(shift_mat, win)
        win_f32 = win.astype(F32)
        for cb in range(lane_blocks):
            lanes = slice(cb * V7X_LANES, (cb + 1) * V7X_LANES)
            acc = jnp.zeros((rows, V7X_LANES), F32)
            for sub in range(V7X_SUBLANES):
                base = (sub - 1) * win_rows
                for grp in range(row_groups):
                    tap = V7X_SUBLANES * grp + sub - (CONV_HALO - CONV_PAD)
                    if 0 <= tap < CONV_WIDTH:
                        lo = V7X_SUBLANES * grp
                        src = (win_f32[lo:lo + rows, lanes] if sub == 0
                               else shifted_all[base + lo:base + lo + rows, lanes])
                        acc = acc + src * w_ref[tap:tap + 1, lanes]
            tile_ref[:, lanes] = acc + b_ref[:, lanes]
        y = _layer_norm(tile_ref[...], g_ref[...], beta_ref[...])
        o_ref[pl.ds(r0, rows), :] = (y * jax.nn.sigmoid(y)).astype(BF16)

    def step(i, carry):
        for u in range(tiles_per_step):
            conv_tile(pl.multiple_of((i * tiles_per_step + u) * rows, rows), conv_ref.at[u])
        return carry

    lax.fori_loop(0, seq // (rows * tiles_per_step), step, 0)


def _conv_call(z, w_dw, b_dw, ln_g, ln_b, batch, seq, rows=64, tiles_per_step=4):
    t = z.shape[0]
    return pl.pallas_call(
        functools.partial(_conv_kernel, seq=seq, rows=rows, tiles_per_step=tiles_per_step),
        grid=(batch,),
        in_specs=[
            pl.BlockSpec((seq, CONV_CH), lambda b: (b, 0)),
            _resident((CONV_WIDTH, CONV_CH)),
            _resident((1, CONV_CH)),
            _resident((1, CONV_CH)),
            _resident((1, CONV_CH)),
        ],
        out_specs=pl.BlockSpec((seq, CONV_CH), lambda b: (b, 0)),
        out_shape=jax.ShapeDtypeStruct((t, CONV_CH), BF16),
        scratch_shapes=[pltpu.VMEM((seq + 2 * CONV_HALO, CONV_CH), BF16),
                        pltpu.VMEM((tiles_per_step, rows, CONV_CH), F32)],
        compiler_params=_params(("parallel",)),
        name="conv_ln_swish",
    )(z, w_dw, b_dw, ln_g, ln_b)


def _merge_kernel(a_ref, c_ref, s_ref, gate_ref, x_ref, wa_ref, wc_ref, ws_ref, wo_ref, o_ref):
    merged = gate_ref[:, 0:D_MODEL].astype(F32) * _dot(a_ref[...], wa_ref[...])
    merged = merged + gate_ref[:, D_MODEL:2 * D_MODEL].astype(F32) * _dot(c_ref[...], wc_ref[...])
    merged = merged + gate_ref[:, 2 * D_MODEL:].astype(F32) * _dot(s_ref[...], ws_ref[...])
    o_ref[...] = x_ref[...] + _dot(merged.astype(BF16), wo_ref[...])


def _merge_call(attn, conv, sgu, gates, x2, wa, wc, ws, wo, tm=256):
    t = x2.shape[0]
    half = D_MODEL // 2
    return pl.pallas_call(
        _merge_kernel,
        grid=(t // tm,),
        in_specs=[
            pl.BlockSpec((tm, half), lambda i: (i, 0)),
            pl.BlockSpec((tm, half), lambda i: (i, 0)),
            pl.BlockSpec((tm, half), lambda i: (i, 0)),
            pl.BlockSpec((tm, N_BRANCH * D_MODEL), lambda i: (i, 0)),
            pl.BlockSpec((tm, D_MODEL), lambda i: (i, 0)),
            _resident((half, D_MODEL)),
            _resident((half, D_MODEL)),
            _resident((half, D_MODEL)),
            _resident((D_MODEL, D_MODEL)),
        ],
        out_specs=pl.BlockSpec((tm, D_MODEL), lambda i: (i, 0)),
        out_shape=jax.ShapeDtypeStruct((t, D_MODEL), F32),
        compiler_params=_params(("parallel",)),
        name="merge_out_proj",
    )(attn, conv, sgu, gates, x2, wa, wc, ws, wo)


def _ffn_kernel(x_ref, g_ref, wg_ref, wu_ref, wd_ref, gf_ref, o_ref, h_ref, *, final_norm):
    j = pl.program_id(1)

    @pl.when(j == 0)
    def _():
        x = x_ref[...]
        h_ref[...] = _rms(x, g_ref[...]).astype(BF16)
        o_ref[...] = x

    h = h_ref[...]
    a = _dot(h, wg_ref[...])
    b = _dot(h, wu_ref[...])
    mid = (a * jax.nn.sigmoid(a) * b).astype(BF16)
    o_ref[...] += _dot(mid, wd_ref[...])

    if final_norm:
        @pl.when(j == pl.num_programs(1) - 1)
        def _():
            o_ref[...] = _rms(o_ref[...], gf_ref[...])


def _ffn_call(x2, g, wg, wu, wd, g_final, final_norm, tm=1024, tf=512):
    t = x2.shape[0]
    d_ff = wg.shape[1]
    return pl.pallas_call(
        functools.partial(_ffn_kernel, final_norm=final_norm),
        grid=(t // tm, d_ff // tf),
        in_specs=[
            pl.BlockSpec((tm, D_MODEL), lambda i, j: (i, 0)),
            _resident((1, D_MODEL)),
            pl.BlockSpec((D_MODEL, tf), lambda i, j: (0, j)),
            pl.BlockSpec((D_MODEL, tf), lambda i, j: (0, j)),
            pl.BlockSpec((tf, D_MODEL), lambda i, j: (j, 0)),
            _resident((1, D_MODEL)),
        ],
        out_specs=pl.BlockSpec((tm, D_MODEL), lambda i, j: (i, 0)),
        out_shape=jax.ShapeDtypeStruct((t, D_MODEL), F32),
        scratch_shapes=[pltpu.VMEM((tm, D_MODEL), BF16)],
        compiler_params=_params(("parallel", "arbitrary")),
        name="swiglu_ffn",
    )(x2, g, wg, wu, wd, g_final)


def _rope_tables(seq):
    pos = jnp.arange(seq, dtype=jnp.int32)
    row = (pos // GRID_W).astype(F32)
    col = (pos % GRID_W).astype(F32)
    inv = ROPE_THETA ** (-jnp.arange(ROPE_FREQ_PER_AXIS, dtype=F32) / ROPE_FREQ_PER_AXIS)
    ang = jnp.concatenate([row[:, None] * inv, col[:, None] * inv], axis=-1)
    cos, sin = jnp.cos(ang), jnp.sin(ang)
    return jnp.concatenate([cos, cos], axis=-1), jnp.concatenate([-sin, sin], axis=-1)


def _gained_tables(cos2, sin2, gain, scale):
    return cos2 * (gain * scale), sin2 * (jnp.roll(gain, ROPE_PAIRS) * scale)


def kernel(x, g_mix, w_in, b_gate, q_norm_g, k_norm_g, w_attn_o, w_dw, b_dw, conv_ln_g, conv_ln_b, w_conv_o, sg_ln_g, sg_ln_b, w_s, b_s, w_sg_o, w_out, g_ffn, w_ff_gate, w_ff_up, w_ff_down, g_final):
    batch, seq, _ = x.shape
    depth = w_in.shape[0]
    cos2, sin2 = _rope_tables(seq)
    x2 = x.reshape(batch * seq, D_MODEL)
    row = lambda p: p.reshape(1, -1)
    c0 = QKV_COLS
    c1 = c0 + CONV_CH
    c2 = c1 + CONV_CH
    c3 = c2 + SG_CH
    c4 = c3 + SG_CH

    for l in range(depth):
        h, qkv = _qkv_call(x2, row(g_mix[l]), _cast_call(w_in, l, 0, c0))
        z = _glu_call(h, _cast_call(w_in, l, c0, CONV_CH), _cast_call(w_in, l, c1, CONV_CH))
        sgu = _sgu_call(h, _cast_call(w_in, l, c2, SG_CH), _cast_call(w_in, l, c3, SG_CH),
                        row(sg_ln_g[l]), row(sg_ln_b[l]), w_s[l], b_s[l].T)
        gates = _gates_call(h, _cast_call(w_in, l, c4), row(b_gate[l]))
        tables = (*_gained_tables(cos2, sin2, q_norm_g[l], HEAD_DIM ** -0.5 * LOG2_E),
                  *_gained_tables(cos2, sin2, k_norm_g[l], 1.0))
        attn = _attn_call(qkv, tables, batch, seq)
        conv = _conv_call(z, w_dw[l].reshape(CONV_WIDTH, CONV_CH), row(b_dw[l]),
                          row(conv_ln_g[l]), row(conv_ln_b[l]), batch, seq)
        x2 = _merge_call(attn, conv, sgu, gates, x2, _cast_call(w_attn_o, l), _cast_call(w_conv_o, l),
                         _cast_call(w_sg_o, l), _cast_call(w_out, l))
        x2 = _ffn_call(x2, row(g_ffn[l]), _cast_call(w_ff_gate, l), _cast_call(w_ff_up, l),
                       _cast_call(w_ff_down, l), row(g_final), final_norm=(l == depth - 1))
    return x2.reshape(batch, seq, D_MODEL)
```

```python
import functools

import jax
import jax.numpy as jnp
from jax import lax
from jax.experimental import pallas as pl
from jax.experimental.pallas import tpu as pltpu

D_MODEL = 2048
GRID_W = 64
HEAD_DIM = 128
N_Q_HEADS = 8
N_KV_HEADS = 2
Q_PER_KV = N_Q_HEADS // N_KV_HEADS
ROPE_THETA = 10000.0
ROPE_PAIRS = HEAD_DIM // 2
ROPE_FREQ_PER_AXIS = ROPE_PAIRS // 2
CONV_CH = 1024
CONV_WIDTH = 31
CONV_PAD = CONV_WIDTH // 2
SG_CH = 1024
SG_GROUP_CH = 128
SG_GROUPS = SG_CH // SG_GROUP_CH
SG_CHUNK = 128
N_BRANCH = 3
Q_COLS = N_Q_HEADS * HEAD_DIM
KV_COLS = N_KV_HEADS * HEAD_DIM
QKV_COLS = Q_COLS + 2 * KV_COLS
RMS_EPS = 1e-6
LN_EPS = 1e-5
LOG2_E = 1.4426950408889634
IN_COLS = QKV_COLS + 2 * CONV_CH + 2 * SG_CH + N_BRANCH * D_MODEL
W_IN_COL_BLOCK = 512

V7X_VMEM_BYTES = 64 * 1024 * 1024
V7X_SUBLANES = 8
V7X_LANES = 128
BF16_SUBLANES = 16
CONV_HALO = -(-CONV_PAD // V7X_SUBLANES) * V7X_SUBLANES

BF16 = jnp.bfloat16
F32 = jnp.float32


def _params(semantics):
    return pltpu.CompilerParams(dimension_semantics=semantics, vmem_limit_bytes=V7X_VMEM_BYTES)


def _resident(shape):
    return pl.BlockSpec(shape, lambda *_: (0,) * len(shape), pipeline_mode=pl.Buffered(1))


def _w_in_cols(first_block, n_blocks):
    return [pl.BlockSpec((D_MODEL, W_IN_COL_BLOCK), functools.partial(lambda j, *_: (0, j), first_block + k),
                         pipeline_mode=pl.Buffered(1)) for k in range(n_blocks)]


def _rms(xf, gain):
    return xf * lax.rsqrt(jnp.mean(xf * xf, axis=-1, keepdims=True) + RMS_EPS) * gain


def _layer_norm(xf, gain, bias):
    mu = jnp.mean(xf, axis=-1, keepdims=True)
    xc = xf - mu
    return xc * lax.rsqrt(jnp.mean(xc * xc, axis=-1, keepdims=True) + LN_EPS) * gain + bias


def _dot(a, b):
    return jnp.dot(a, b, preferred_element_type=F32)


CAST_COL_BLOCK = 512
CAST_MAX_ROW_BLOCK = 2048


def _cast_kernel(w_ref, o_ref):
    o_ref[...] = w_ref[...].astype(BF16)


def _cast_call(w, layer):
    _, rows, cols = w.shape
    assert cols % CAST_COL_BLOCK == 0
    row_blk = next(r for r in range(min(rows, CAST_MAX_ROW_BLOCK), 0, -V7X_SUBLANES) if rows % r == 0)
    return pl.pallas_call(
        _cast_kernel,
        grid=(rows // row_blk, cols // CAST_COL_BLOCK),
        in_specs=[pl.BlockSpec((None, row_blk, CAST_COL_BLOCK), lambda i, j: (layer, i, j))],
        out_specs=pl.BlockSpec((row_blk, CAST_COL_BLOCK), lambda i, j: (i, j)),
        out_shape=jax.ShapeDtypeStruct((rows, cols), BF16),
        compiler_params=_params(("parallel", "parallel")),
        name="weight_to_bf16",
    )(w)


def _qkv_kernel(x_ref, g_ref, w_ref, h_ref, qkv_ref):
    h = _rms(x_ref[...], g_ref[...]).astype(BF16)
    h_ref[...] = h
    qkv_ref[...] = _dot(h, w_ref[...]).astype(BF16)


def _qkv_call(x2, g, w, tm=512):
    t = x2.shape[0]
    return pl.pallas_call(
        _qkv_kernel,
        grid=(t // tm,),
        in_specs=[
            pl.BlockSpec((tm, D_MODEL), lambda i: (i, 0)),
            _resident((1, D_MODEL)),
            pl.BlockSpec((D_MODEL, QKV_COLS), lambda i: (0, 0), pipeline_mode=pl.Buffered(1)),
        ],
        out_specs=[
            pl.BlockSpec((tm, D_MODEL), lambda i: (i, 0)),
            pl.BlockSpec((tm, QKV_COLS), lambda i: (i, 0)),
        ],
        out_shape=[
            jax.ShapeDtypeStruct((t, D_MODEL), BF16),
            jax.ShapeDtypeStruct((t, QKV_COLS), BF16),
        ],
        compiler_params=_params(("parallel",)),
        name="rms_qkv",
    )(x2, g, w)


def _glu_kernel(h_ref, *refs):
    *w_refs, z_ref = refs
    n = len(w_refs) // 2
    h = h_ref[...]
    for k in range(n):
        cols = slice(k * W_IN_COL_BLOCK, (k + 1) * W_IN_COL_BLOCK)
        z_ref[:, cols] = (_dot(h, w_refs[k][...]) * jax.nn.sigmoid(_dot(h, w_refs[n + k][...]))).astype(BF16)


def _glu_call(h, w_in_bf16, tm=1024):
    t = h.shape[0]
    first = QKV_COLS // W_IN_COL_BLOCK
    n_blocks = 2 * CONV_CH // W_IN_COL_BLOCK
    return pl.pallas_call(
        _glu_kernel,
        grid=(t // tm,),
        in_specs=[pl.BlockSpec((tm, D_MODEL), lambda i: (i, 0)), *_w_in_cols(first, n_blocks)],
        out_specs=pl.BlockSpec((tm, CONV_CH), lambda i: (i, 0)),
        out_shape=jax.ShapeDtypeStruct((t, CONV_CH), BF16),
        compiler_params=_params(("parallel",)),
        name="conv_glu_in",
    )(h, *([w_in_bf16] * n_blocks))


def _sgu_kernel(h_ref, *refs):
    *w_refs, g_ref, b_ref, ws_ref, bst_ref, o_ref = refs
    n = len(w_refs) // 2
    h = h_ref[...]
    project = lambda blocks: jax.nn.gelu(jnp.concatenate([_dot(h, w[...]) for w in blocks], axis=1))
    u = project(w_refs[:n])
    v = project(w_refs[n:])
    for grp in range(SG_GROUPS):
        cols = slice(grp * SG_GROUP_CH, (grp + 1) * SG_GROUP_CH)
        vn = _layer_norm(v[:, cols], g_ref[:, cols], b_ref[:, cols]).astype(BF16)
        w = ws_ref[grp].astype(BF16)
        bias = bst_ref[:, grp:grp + 1]
        for r0 in range(0, h.shape[0], SG_CHUNK):
            rows = slice(r0, r0 + SG_CHUNK)
            o_ref[rows, cols] = (u[rows, cols] * (_dot(w, vn[rows, :]) + bias)).astype(BF16)


def _sgu_call(h, w_in_bf16, ln_g, ln_b, ws, bst, tm=1024):
    t = h.shape[0]
    assert tm % SG_CHUNK == 0
    first = (QKV_COLS + 2 * CONV_CH) // W_IN_COL_BLOCK
    n_blocks = 2 * SG_CH // W_IN_COL_BLOCK
    return pl.pallas_call(
        _sgu_kernel,
        grid=(t // tm,),
        in_specs=[
            pl.BlockSpec((tm, D_MODEL), lambda i: (i, 0)),
            *_w_in_cols(first, n_blocks),
            _resident((1, SG_CH)),
            _resident((1, SG_CH)),
            _resident((SG_GROUPS, SG_CHUNK, SG_CHUNK)),
            _resident((SG_CHUNK, SG_GROUPS)),
        ],
        out_specs=pl.BlockSpec((tm, SG_CH), lambda i: (i, 0)),
        out_shape=jax.ShapeDtypeStruct((t, SG_CH), BF16),
        compiler_params=_params(("parallel",)),
        name="spatial_gating",
    )(h, *([w_in_bf16] * n_blocks), ln_g, ln_b, ws, bst)


def _gates_kernel(h_ref, *refs):
    *w_refs, b_ref, o_ref = refs
    h = h_ref[...]
    for k, w_ref in enumerate(w_refs):
        cols = slice(k * W_IN_COL_BLOCK, (k + 1) * W_IN_COL_BLOCK)
        o_ref[:, cols] = jax.nn.sigmoid(_dot(h, w_ref[...]) + b_ref[:, cols]).astype(BF16)


def _gates_call(h, w_in_bf16, b, tm=512):
    t = h.shape[0]
    n = N_BRANCH * D_MODEL
    first = (IN_COLS - n) // W_IN_COL_BLOCK
    n_blocks = n // W_IN_COL_BLOCK
    return pl.pallas_call(
        _gates_kernel,
        grid=(t // tm,),
        in_specs=[pl.BlockSpec((tm, D_MODEL), lambda i: (i, 0)), *_w_in_cols(first, n_blocks),
                  _resident((1, n))],
        out_specs=pl.BlockSpec((tm, n), lambda i: (i, 0)),
        out_shape=jax.ShapeDtypeStruct((t, n), BF16),
        compiler_params=_params(("parallel",)),
        name="branch_gates",
    )(h, *([w_in_bf16] * n_blocks), b)


def _norm_rope(y, cos, sin):
    y = y.astype(F32)
    rinv = lax.rsqrt(jnp.mean(y * y, axis=-1, keepdims=True) + RMS_EPS)
    return ((y * cos + pltpu.roll(y, ROPE_PAIRS, axis=1) * sin) * rinv).astype(BF16)


def _attn_kernel(q_ref, k_ref, v_ref, qcos_ref, qsin_ref, kcos_ref, ksin_ref, *rest, sub_rows, n_ride):
    ride_in, (o_ref, *ride_out), k_scr = rest[:n_ride], rest[n_ride:-1], rest[-1]
    for src, dst in zip(ride_in, ride_out):
        dst[...] = src[...].astype(BF16)

    @pl.when(pl.program_id(2) == 0)
    def _():
        k_scr[...] = _norm_rope(k_ref[...], kcos_ref[...], ksin_ref[...])

    k = k_scr[...]
    v = v_ref[...]
    v_ext = jnp.concatenate([v, jnp.ones_like(v)], axis=1)
    tq = q_ref.shape[0]
    for r0 in range(0, tq, sub_rows):
        rows = slice(r0, r0 + sub_rows)
        for grp in range(Q_PER_KV):
            lo = grp * HEAD_DIM
            q = _norm_rope(q_ref[rows, lo:lo + HEAD_DIM], qcos_ref[rows, :], qsin_ref[rows, :])
            s = lax.dot_general(q, k, (((1,), (1,)), ((), ())), preferred_element_type=F32)
            p = jnp.exp2(s - jnp.max(s, axis=-1, keepdims=True)).astype(BF16)
            o = _dot(p, v_ext)
            o_ref[rows, lo:lo + HEAD_DIM] = (o[:, :HEAD_DIM] / o[:, HEAD_DIM:]).astype(BF16)


def _attn_call(qkv, tables, ride, batch, seq, tq=1024):
    t = qkv.shape[0]
    q_blocks = seq // tq
    grp_cols = Q_PER_KV * HEAD_DIM
    k_col0 = Q_COLS // HEAD_DIM
    v_col0 = (Q_COLS + KV_COLS) // HEAD_DIM
    q_table = pl.BlockSpec((tq, HEAD_DIM), lambda b, h, i: (i, 0))
    k_table = pl.BlockSpec((seq, HEAD_DIM), lambda b, h, i: (0, 0))
    n_steps = batch * N_KV_HEADS * q_blocks
    step = lambda b, h, i: (b * N_KV_HEADS + h) * q_blocks + i
    ride_in, ride_out, ride_shapes = [], [], []
    for w, layer in ride:
        _, rows, cols = w.shape
        slab = rows // n_steps
        assert slab * n_steps == rows and slab % BF16_SUBLANES == 0
        ride_in.append(pl.BlockSpec((None, slab, cols), functools.partial(
            lambda layer, b, h, i: (layer, step(b, h, i), 0), layer)))
        ride_out.append(pl.BlockSpec((slab, cols), lambda b, h, i: (step(b, h, i), 0)))
        ride_shapes.append(jax.ShapeDtypeStruct((rows, cols), BF16))
    out = pl.pallas_call(
        functools.partial(_attn_kernel, sub_rows=256, n_ride=len(ride)),
        grid=(batch, N_KV_HEADS, q_blocks),
        in_specs=[
            pl.BlockSpec((tq, grp_cols), lambda b, h, i: (b * q_blocks + i, h)),
            pl.BlockSpec((seq, HEAD_DIM), lambda b, h, i: (b, k_col0 + h)),
            pl.BlockSpec((seq, HEAD_DIM), lambda b, h, i: (b, v_col0 + h)),
            q_table, q_table, k_table, k_table, *ride_in,
        ],
        out_specs=[pl.BlockSpec((tq, grp_cols), lambda b, h, i: (b * q_blocks + i, h)), *ride_out],
        out_shape=[jax.ShapeDtypeStruct((t, Q_COLS), BF16), *ride_shapes],
        scratch_shapes=[pltpu.VMEM((seq, HEAD_DIM), BF16)],
        compiler_params=_params(("arbitrary", "arbitrary", "arbitrary")),
        name="gqa_attention",
    )(qkv, qkv, qkv, *tables, *(w for w, _ in ride))
    return out[0], out[1:]


def _conv_kernel(z_ref, w_ref, b_ref, g_ref, beta_ref, o_ref, pad_ref, conv_ref, *, seq, rows,
                 tiles_per_step):
    zeros = jnp.zeros((CONV_HALO, CONV_CH), BF16)
    pad_ref[0:CONV_HALO, :] = zeros
    pad_ref[CONV_HALO + seq:, :] = zeros
    pad_ref[CONV_HALO:CONV_HALO + seq, :] = z_ref[...]
    win_rows = rows + 2 * CONV_HALO
    lane_blocks = CONV_CH // V7X_LANES
    row_groups = 2 * CONV_HALO // V7X_SUBLANES
    n_shifts = V7X_SUBLANES - 1
    r = lax.broadcasted_iota(jnp.int32, (n_shifts * win_rows, win_rows), 0)
    c = lax.broadcasted_iota(jnp.int32, (n_shifts * win_rows, win_rows), 1)
    shift_mat = (c == r % win_rows + r // win_rows + 1).astype(BF16)

    def conv_tile(r0, tile_ref):
        win = pad_ref[pl.ds(r0, win_rows), :]
        shifted_all = _dot(shift_mat, win)
        win_f32 = win.astype(F32)
        for cb in range(lane_blocks):
            lanes = slice(cb * V7X_LANES, (cb + 1) * V7X_LANES)
            acc = jnp.zeros((rows, V7X_LANES), F32)
            for sub in range(V7X_SUBLANES):
                base = (sub - 1) * win_rows
                for grp in range(row_groups):
                    tap = V7X_SUBLANES * grp + sub - (CONV_HALO - CONV_PAD)
                    if 0 <= tap < CONV_WIDTH:
                        lo = V7X_SUBLANES * grp
                        src = (win_f32[lo:lo + rows, lanes] if sub == 0
                               else shifted_all[base + lo:base + lo + rows, lanes])
                        acc = acc + src * w_ref[tap:tap + 1, lanes]
            tile_ref[:, lanes] = acc + b_ref[:, lanes]
        y = _layer_norm(tile_ref[...], g_ref[...], beta_ref[...])
        o_ref[pl.ds(r0, rows), :] = (y * jax.nn.sigmoid(y)).astype(BF16)

    def step(i, carry):
        for u in range(tiles_per_step):
            conv_tile(pl.multiple_of((i * tiles_per_step + u) * rows, rows), conv_ref.at[u])
        return carry

    lax.fori_loop(0, seq // (rows * tiles_per_step), step, 0)


def _conv_call(z, w_dw, b_dw, ln_g, ln_b, batch, seq, rows=64, tiles_per_step=4):
    t = z.shape[0]
    return pl.pallas_call(
        functools.partial(_conv_kernel, seq=seq, rows=rows, tiles_per_step=tiles_per_step),
        grid=(batch,),
        in_specs=[
            pl.BlockSpec((seq, CONV_CH), lambda b: (b, 0)),
            _resident((CONV_WIDTH, CONV_CH)),
            _resident((1, CONV_CH)),
            _resident((1, CONV_CH)),
            _resident((1, CONV_CH)),
        ],
        out_specs=pl.BlockSpec((seq, CONV_CH), lambda b: (b, 0)),
        out_shape=jax.ShapeDtypeStruct((t, CONV_CH), BF16),
        scratch_shapes=[pltpu.VMEM((seq + 2 * CONV_HALO, CONV_CH), BF16),
                        pltpu.VMEM((tiles_per_step, rows, CONV_CH), F32)],
        compiler_params=_params(("parallel",)),
        name="conv_ln_swish",
    )(z, w_dw, b_dw, ln_g, ln_b)


def _merge_kernel(a_ref, c_ref, s_ref, gate_ref, x_ref, wa_ref, wc_ref, ws_ref, wo_ref, o_ref):
    merged = gate_ref[:, 0:D_MODEL].astype(F32) * _dot(a_ref[...], wa_ref[...])
    merged = merged + gate_ref[:, D_MODEL:2 * D_MODEL].astype(F32) * _dot(c_ref[...], wc_ref[...])
    merged = merged + gate_ref[:, 2 * D_MODEL:].astype(F32) * _dot(s_ref[...], ws_ref[...])
    o_ref[...] = x_ref[...] + _dot(merged.astype(BF16), wo_ref[...])


def _merge_call(attn, conv, sgu, gates, x2, wa, wc, ws, wo, tm=512):
    t = x2.shape[0]
    half = D_MODEL // 2
    return pl.pallas_call(
        _merge_kernel,
        grid=(t // tm,),
        in_specs=[
            pl.BlockSpec((tm, half), lambda i: (i, 0)),
            pl.BlockSpec((tm, half), lambda i: (i, 0)),
            pl.BlockSpec((tm, half), lambda i: (i, 0)),
            pl.BlockSpec((tm, N_BRANCH * D_MODEL), lambda i: (i, 0)),
            pl.BlockSpec((tm, D_MODEL), lambda i: (i, 0)),
            _resident((half, D_MODEL)),
            _resident((half, D_MODEL)),
            _resident((half, D_MODEL)),
            _resident((D_MODEL, D_MODEL)),
        ],
        out_specs=pl.BlockSpec((tm, D_MODEL), lambda i: (i, 0)),
        out_shape=jax.ShapeDtypeStruct((t, D_MODEL), F32),
        compiler_params=_params(("parallel",)),
        name="merge_out_proj",
    )(attn, conv, sgu, gates, x2, wa, wc, ws, wo)


def _ffn_kernel(x_ref, g_ref, wg_ref, wu_ref, wd_ref, gf_ref, o_ref, h_ref, *, final_norm):
    j = pl.program_id(1)

    @pl.when(j == 0)
    def _():
        x = x_ref[...]
        h_ref[...] = _rms(x, g_ref[...]).astype(BF16)
        o_ref[...] = x

    h = h_ref[...]
    a = _dot(h, wg_ref[...])
    b = _dot(h, wu_ref[...])
    mid = (a * jax.nn.sigmoid(a) * b).astype(BF16)
    o_ref[...] += _dot(mid, wd_ref[...])

    if final_norm:
        @pl.when(j == pl.num_programs(1) - 1)
        def _():
            o_ref[...] = _rms(o_ref[...], gf_ref[...])


def _ffn_call(x2, g, wg, wu, wd, g_final, final_norm, tm=1024, tf=512):
    t = x2.shape[0]
    d_ff = wg.shape[1]
    return pl.pallas_call(
        functools.partial(_ffn_kernel, final_norm=final_norm),
        grid=(t // tm, d_ff // tf),
        in_specs=[
            pl.BlockSpec((tm, D_MODEL), lambda i, j: (i, 0)),
            _resident((1, D_MODEL)),
            pl.BlockSpec((D_MODEL, tf), lambda i, j: (0, j)),
            pl.BlockSpec((D_MODEL, tf), lambda i, j: (0, j)),
            pl.BlockSpec((tf, D_MODEL), lambda i, j: (j, 0)),
            _resident((1, D_MODEL)),
        ],
        out_specs=pl.BlockSpec((tm, D_MODEL), lambda i, j: (i, 0)),
        out_shape=jax.ShapeDtypeStruct((t, D_MODEL), F32),
        scratch_shapes=[pltpu.VMEM((tm, D_MODEL), BF16)],
        compiler_params=_params(("parallel", "arbitrary")),
        name="swiglu_ffn",
    )(x2, g, wg, wu, wd, g_final)


def _rope_tables(seq):
    pos = jnp.arange(seq, dtype=jnp.int32)
    row = (pos // GRID_W).astype(F32)
    col = (pos % GRID_W).astype(F32)
    inv = ROPE_THETA ** (-jnp.arange(ROPE_FREQ_PER_AXIS, dtype=F32) / ROPE_FREQ_PER_AXIS)
    ang = jnp.concatenate([row[:, None] * inv, col[:, None] * inv], axis=-1)
    cos, sin = jnp.cos(ang), jnp.sin(ang)
    return jnp.concatenate([cos, cos], axis=-1), jnp.concatenate([-sin, sin], axis=-1)


def _gained_tables(cos2, sin2, gain, scale):
    return cos2 * (gain * scale), sin2 * (jnp.roll(gain, ROPE_PAIRS) * scale)


def kernel(x, g_mix, w_in, b_gate, q_norm_g, k_norm_g, w_attn_o, w_dw, b_dw, conv_ln_g, conv_ln_b, w_conv_o, sg_ln_g, sg_ln_b, w_s, b_s, w_sg_o, w_out, g_ffn, w_ff_gate, w_ff_up, w_ff_down, g_final):
    batch, seq, _ = x.shape
    depth = w_in.shape[0]
    cos2, sin2 = _rope_tables(seq)
    x2 = x.reshape(batch * seq, D_MODEL)
    row = lambda p: p.reshape(1, -1)
    w_in_bf16 = _cast_call(w_in, 0)
    for l in range(depth):
        h, qkv = _qkv_call(x2, row(g_mix[l]), w_in_bf16)
        z = _glu_call(h, w_in_bf16)
        sgu = _sgu_call(h, w_in_bf16, row(sg_ln_g[l]), row(sg_ln_b[l]), w_s[l], b_s[l].T)
        gates = _gates_call(h, w_in_bf16, row(b_gate[l]))
        tables = (*_gained_tables(cos2, sin2, q_norm_g[l], HEAD_DIM ** -0.5 * LOG2_E),
                  *_gained_tables(cos2, sin2, k_norm_g[l], 1.0))
        ride = [(w, l) for w in (w_attn_o, w_conv_o, w_sg_o, w_out, w_ff_gate, w_ff_up, w_ff_down)]
        if l + 1 < depth:
            ride.append((w_in, l + 1))
        attn, staged = _attn_call(qkv, tables, ride, batch, seq)
        w_a, w_c, w_s_o, w_o, w_fg, w_fu, w_fd = staged[:7]
        conv = _conv_call(z, w_dw[l].reshape(CONV_WIDTH, CONV_CH), row(b_dw[l]),
                          row(conv_ln_g[l]), row(conv_ln_b[l]), batch, seq)
        x2 = _merge_call(attn, conv, sgu, gates, x2, w_a, w_c, w_s_o, w_o)
        x2 = _ffn_call(x2, row(g_ffn[l]), w_fg, w_fu, w_fd, row(g_final), final_norm=(l == depth - 1))
        if l + 1 < depth:
            w_in_bf16 = staged[7]
    return x2.reshape(batch, seq, D_MODEL)
```

```python
import functools

import jax
import jax.numpy as jnp
from jax import lax
from jax.experimental import pallas as pl
from jax.experimental.pallas import tpu as pltpu

D_MODEL = 2048
GRID_W = 64
HEAD_DIM = 128
N_Q_HEADS = 8
N_KV_HEADS = 2
Q_PER_KV = N_Q_HEADS // N_KV_HEADS
ROPE_THETA = 10000.0
ROPE_PAIRS = HEAD_DIM // 2
ROPE_FREQ_PER_AXIS = ROPE_PAIRS // 2
CONV_CH = 1024
CONV_WIDTH = 31
CONV_PAD = CONV_WIDTH // 2
SG_CH = 1024
SG_GROUP_CH = 128
SG_GROUPS = SG_CH // SG_GROUP_CH
SG_CHUNK = 128
N_BRANCH = 3
Q_COLS = N_Q_HEADS * HEAD_DIM
KV_COLS = N_KV_HEADS * HEAD_DIM
QKV_COLS = Q_COLS + 2 * KV_COLS
RMS_EPS = 1e-6
LN_EPS = 1e-5
LOG2_E = 1.4426950408889634
IN_COLS = QKV_COLS + 2 * CONV_CH + 2 * SG_CH + N_BRANCH * D_MODEL
W_IN_COL_BLOCK = 512

V7X_VMEM_BYTES = 64 * 1024 * 1024
V7X_SUBLANES = 8
V7X_LANES = 128
BF16_SUBLANES = 16
CONV_HALO = -(-CONV_PAD // V7X_SUBLANES) * V7X_SUBLANES

BF16 = jnp.bfloat16
F32 = jnp.float32


def _params(semantics):
    return pltpu.CompilerParams(dimension_semantics=semantics, vmem_limit_bytes=V7X_VMEM_BYTES)


def _resident(shape):
    return pl.BlockSpec(shape, lambda *_: (0,) * len(shape), pipeline_mode=pl.Buffered(1))


def _w_in_cols(first_block, n_blocks):
    return [pl.BlockSpec((D_MODEL, W_IN_COL_BLOCK), functools.partial(lambda j, *_: (0, j), first_block + k),
                         pipeline_mode=pl.Buffered(1)) for k in range(n_blocks)]


def _rms(xf, gain):
    return xf * lax.rsqrt(jnp.mean(xf * xf, axis=-1, keepdims=True) + RMS_EPS) * gain


def _layer_norm(xf, gain, bias):
    mu = jnp.mean(xf, axis=-1, keepdims=True)
    xc = xf - mu
    return xc * lax.rsqrt(jnp.mean(xc * xc, axis=-1, keepdims=True) + LN_EPS) * gain + bias


def _dot(a, b):
    return jnp.dot(a, b, preferred_element_type=F32)


CAST_COL_BLOCK = 512
CAST_MAX_ROW_BLOCK = 2048


def _cast_kernel(w_ref, o_ref):
    o_ref[...] = w_ref[...].astype(BF16)


def _cast_call(w, layer):
    _, rows, cols = w.shape
    assert cols % CAST_COL_BLOCK == 0
    row_blk = next(r for r in range(min(rows, CAST_MAX_ROW_BLOCK), 0, -V7X_SUBLANES) if rows % r == 0)
    return pl.pallas_call(
        _cast_kernel,
        grid=(rows // row_blk, cols // CAST_COL_BLOCK),
        in_specs=[pl.BlockSpec((None, row_blk, CAST_COL_BLOCK), lambda i, j: (layer, i, j))],
        out_specs=pl.BlockSpec((row_blk, CAST_COL_BLOCK), lambda i, j: (i, j)),
        out_shape=jax.ShapeDtypeStruct((rows, cols), BF16),
        compiler_params=_params(("parallel", "parallel")),
        name="weight_to_bf16",
    )(w)


def _qkv_kernel(x_ref, g_ref, w_ref, h_ref, qkv_ref):
    h = _rms(x_ref[...], g_ref[...]).astype(BF16)
    h_ref[...] = h
    qkv_ref[...] = _dot(h, w_ref[...]).astype(BF16)


def _qkv_call(x2, g, w, tm=1024):
    t = x2.shape[0]
    return pl.pallas_call(
        _qkv_kernel,
        grid=(t // tm,),
        in_specs=[
            pl.BlockSpec((tm, D_MODEL), lambda i: (i, 0)),
            _resident((1, D_MODEL)),
            pl.BlockSpec((D_MODEL, QKV_COLS), lambda i: (0, 0), pipeline_mode=pl.Buffered(1)),
        ],
        out_specs=[
            pl.BlockSpec((tm, D_MODEL), lambda i: (i, 0)),
            pl.BlockSpec((tm, QKV_COLS), lambda i: (i, 0)),
        ],
        out_shape=[
            jax.ShapeDtypeStruct((t, D_MODEL), BF16),
            jax.ShapeDtypeStruct((t, QKV_COLS), BF16),
        ],
        compiler_params=_params(("parallel",)),
        name="rms_qkv",
    )(x2, g, w)


def _glu_kernel(h_ref, *refs):
    *w_refs, z_ref = refs
    n = len(w_refs) // 2
    h = h_ref[...]
    for k in range(n):
        cols = slice(k * W_IN_COL_BLOCK, (k + 1) * W_IN_COL_BLOCK)
        z_ref[:, cols] = (_dot(h, w_refs[k][...]) * jax.nn.sigmoid(_dot(h, w_refs[n + k][...]))).astype(BF16)


def _glu_call(h, w_in_bf16, tm=1024):
    t = h.shape[0]
    first = QKV_COLS // W_IN_COL_BLOCK
    n_blocks = 2 * CONV_CH // W_IN_COL_BLOCK
    return pl.pallas_call(
        _glu_kernel,
        grid=(t // tm,),
        in_specs=[pl.BlockSpec((tm, D_MODEL), lambda i: (i, 0)), *_w_in_cols(first, n_blocks)],
        out_specs=pl.BlockSpec((tm, CONV_CH), lambda i: (i, 0)),
        out_shape=jax.ShapeDtypeStruct((t, CONV_CH), BF16),
        compiler_params=_params(("parallel",)),
        name="conv_glu_in",
    )(h, *([w_in_bf16] * n_blocks))


def _sgu_kernel(h_ref, *refs):
    *w_refs, g_ref, b_ref, ws_ref, bst_ref, o_ref = refs
    n = len(w_refs) // 2
    h = h_ref[...]
    project = lambda blocks: jax.nn.gelu(jnp.concatenate([_dot(h, w[...]) for w in blocks], axis=1))
    u = project(w_refs[:n])
    v = project(w_refs[n:])
    for grp in range(SG_GROUPS):
        cols = slice(grp * SG_GROUP_CH, (grp + 1) * SG_GROUP_CH)
        vn = _layer_norm(v[:, cols], g_ref[:, cols], b_ref[:, cols]).astype(BF16)
        w = ws_ref[grp].astype(BF16)
        bias = bst_ref[:, grp:grp + 1]
        for r0 in range(0, h.shape[0], SG_CHUNK):
            rows = slice(r0, r0 + SG_CHUNK)
            o_ref[rows, cols] = (u[rows, cols] * (_dot(w, vn[rows, :]) + bias)).astype(BF16)


def _sgu_call(h, w_in_bf16, ln_g, ln_b, ws, bst, tm=1024):
    t = h.shape[0]
    assert tm % SG_CHUNK == 0
    first = (QKV_COLS + 2 * CONV_CH) // W_IN_COL_BLOCK
    n_blocks = 2 * SG_CH // W_IN_COL_BLOCK
    return pl.pallas_call(
        _sgu_kernel,
        grid=(t // tm,),
        in_specs=[
            pl.BlockSpec((tm, D_MODEL), lambda i: (i, 0)),
            *_w_in_cols(first, n_blocks),
            _resident((1, SG_CH)),
            _resident((1, SG_CH)),
            _resident((SG_GROUPS, SG_CHUNK, SG_CHUNK)),
            _resident((SG_CHUNK, SG_GROUPS)),
        ],
        out_specs=pl.BlockSpec((tm, SG_CH), lambda i: (i, 0)),
        out_shape=jax.ShapeDtypeStruct((t, SG_CH), BF16),
        compiler_params=_params(("parallel",)),
        name="spatial_gating",
    )(h, *([w_in_bf16] * n_blocks), ln_g, ln_b, ws, bst)


def _gates_kernel(h_ref, *refs):
    *w_refs, b_ref, o_ref = refs
    h = h_ref[...]
    for k, w_ref in enumerate(w_refs):
        cols = slice(k * W_IN_COL_BLOCK, (k + 1) * W_IN_COL_BLOCK)
        o_ref[:, cols] = jax.nn.sigmoid(_dot(h, w_ref[...]) + b_ref[:, cols]).astype(BF16)


def _gates_call(h, w_in_bf16, b, tm=1024):
    t = h.shape[0]
    n = N_BRANCH * D_MODEL
    first = (IN_COLS - n) // W_IN_COL_BLOCK
    n_blocks = n // W_IN_COL_BLOCK
    return pl.pallas_call(
        _gates_kernel,
        grid=(t // tm,),
        in_specs=[pl.BlockSpec((tm, D_MODEL), lambda i: (i, 0)), *_w_in_cols(first, n_blocks),
                  _resident((1, n))],
        out_specs=pl.BlockSpec((tm, n), lambda i: (i, 0)),
        out_shape=jax.ShapeDtypeStruct((t, n), BF16),
        compiler_params=_params(("parallel",)),
        name="branch_gates",
    )(h, *([w_in_bf16] * n_blocks), b)


def _norm_rope(y, cos, sin):
    y = y.astype(F32)
    rinv = lax.rsqrt(jnp.mean(y * y, axis=-1, keepdims=True) + RMS_EPS)
    return ((y * cos + pltpu.roll(y, ROPE_PAIRS, axis=1) * sin) * rinv).astype(BF16)


def _attn_kernel(q_ref, k_ref, v_ref, qcos_ref, qsin_ref, kcos_ref, ksin_ref, *rest, sub_rows, n_ride):
    ride_in, (o_ref, *ride_out), k_scr = rest[:n_ride], rest[n_ride:-1], rest[-1]
    for src, dst in zip(ride_in, ride_out):
        dst[...] = src[...].astype(BF16)

    @pl.when(pl.program_id(2) == 0)
    def _():
        k_scr[...] = _norm_rope(k_ref[...], kcos_ref[...], ksin_ref[...])

    k = k_scr[...]
    v = v_ref[...]
    v_ext = jnp.concatenate([v, jnp.ones_like(v)], axis=1)
    tq = q_ref.shape[0]
    for r0 in range(0, tq, sub_rows):
        rows = slice(r0, r0 + sub_rows)
        for grp in range(Q_PER_KV):
            lo = grp * HEAD_DIM
            q = _norm_rope(q_ref[rows, lo:lo + HEAD_DIM], qcos_ref[rows, :], qsin_ref[rows, :])
            s = lax.dot_general(q, k, (((1,), (1,)), ((), ())), preferred_element_type=F32)
            p = jnp.exp2(s - jnp.max(s, axis=-1, keepdims=True)).astype(BF16)
            o = _dot(p, v_ext)
            o_ref[rows, lo:lo + HEAD_DIM] = (o[:, :HEAD_DIM] / o[:, HEAD_DIM:]).astype(BF16)


def _attn_call(qkv, tables, ride, batch, seq, tq=1024):
    t = qkv.shape[0]
    q_blocks = seq // tq
    grp_cols = Q_PER_KV * HEAD_DIM
    k_col0 = Q_COLS // HEAD_DIM
    v_col0 = (Q_COLS + KV_COLS) // HEAD_DIM
    q_table = pl.BlockSpec((tq, HEAD_DIM), lambda b, h, i: (i, 0))
    k_table = pl.BlockSpec((seq, HEAD_DIM), lambda b, h, i: (0, 0))
    n_steps = batch * N_KV_HEADS * q_blocks
    step = lambda b, h, i: (b * N_KV_HEADS + h) * q_blocks + i
    ride_in, ride_out, ride_shapes = [], [], []
    for w, layer in ride:
        _, rows, cols = w.shape
        slab = rows // n_steps
        assert slab * n_steps == rows and slab % BF16_SUBLANES == 0
        ride_in.append(pl.BlockSpec((None, slab, cols), functools.partial(
            lambda layer, b, h, i: (layer, step(b, h, i), 0), layer)))
        ride_out.append(pl.BlockSpec((slab, cols), lambda b, h, i: (step(b, h, i), 0)))
        ride_shapes.append(jax.ShapeDtypeStruct((rows, cols), BF16))
    out = pl.pallas_call(
        functools.partial(_attn_kernel, sub_rows=256, n_ride=len(ride)),
        grid=(batch, N_KV_HEADS, q_blocks),
        in_specs=[
            pl.BlockSpec((tq, grp_cols), lambda b, h, i: (b * q_blocks + i, h)),
            pl.BlockSpec((seq, HEAD_DIM), lambda b, h, i: (b, k_col0 + h)),
            pl.BlockSpec((seq, HEAD_DIM), lambda b, h, i: (b, v_col0 + h)),
            q_table, q_table, k_table, k_table, *ride_in,
        ],
        out_specs=[pl.BlockSpec((tq, grp_cols), lambda b, h, i: (b * q_blocks + i, h)), *ride_out],
        out_shape=[jax.ShapeDtypeStruct((t, Q_COLS), BF16), *ride_shapes],
        scratch_shapes=[pltpu.VMEM((seq, HEAD_DIM), BF16)],
        compiler_params=_params(("arbitrary", "arbitrary", "arbitrary")),
        name="gqa_attention",
    )(qkv, qkv, qkv, *tables, *(w for w, _ in ride))
    return out[0], out[1:]


def _conv_kernel(z_ref, w_ref, b_ref, g_ref, beta_ref, o_ref, pad_ref, conv_ref, *, seq, rows,
                 tiles_per_step):
    zeros = jnp.zeros((CONV_HALO, CONV_CH), BF16)
    pad_ref[0:CONV_HALO, :] = zeros
    pad_ref[CONV_HALO + seq:, :] = zeros
    pad_ref[CONV_HALO:CONV_HALO + seq, :] = z_ref[...]
    win_rows = rows + 2 * CONV_HALO
    lane_blocks = CONV_CH // V7X_LANES
    row_groups = 2 * CONV_HALO // V7X_SUBLANES
    n_shifts = V7X_SUBLANES - 1
    r = lax.broadcasted_iota(jnp.int32, (n_shifts * win_rows, win_rows), 0)
    c = lax.broadcasted_iota(jnp.int32, (n_shifts * win_rows, win_rows), 1)
    shift_mat = (c == r % win_rows + r // win_rows + 1).astype(BF16)

    def conv_tile(r0, tile_ref):
        win = pad_ref[pl.ds(r0, win_rows), :]
        shifted_all = _dot(shift_mat, win)
        win_f32 = win.astype(F32)
        for cb in range(lane_blocks):
            lanes = slice(cb * V7X_LANES, (cb + 1) * V7X_LANES)
            acc = jnp.zeros((rows, V7X_LANES), F32)
            for sub in range(V7X_SUBLANES):
                base = (sub - 1) * win_rows
                for grp in range(row_groups):
                    tap = V7X_SUBLANES * grp + sub - (CONV_HALO - CONV_PAD)
                    if 0 <= tap < CONV_WIDTH:
                        lo = V7X_SUBLANES * grp
                        src = (win_f32[lo:lo + rows, lanes] if sub == 0
                               else shifted_all[base + lo:base + lo + rows, lanes])
                        acc = acc + src * w_ref[tap:tap + 1, lanes]
            tile_ref[:, lanes] = acc + b_ref[:, lanes]
        y = _layer_norm(tile_ref[...], g_ref[...], beta_ref[...])
        o_ref[pl.ds(r0, rows), :] = (y * jax.nn.sigmoid(y)).astype(BF16)

    def step(i, carry):
        for u in range(tiles_per_step):
            conv_tile(pl.multiple_of((i * tiles_per_step + u) * rows, rows), conv_ref.at[u])
        return carry

    lax.fori_loop(0, seq // (rows * tiles_per_step), step, 0)


def _conv_call(z, w_dw, b_dw, ln_g, ln_b, batch, seq, rows=64, tiles_per_step=4):
    t = z.shape[0]
    return pl.pallas_call(
        functools.partial(_conv_kernel, seq=seq, rows=rows, tiles_per_step=tiles_per_step),
        grid=(batch,),
        in_specs=[
            pl.BlockSpec((seq, CONV_CH), lambda b: (b, 0)),
            _resident((CONV_WIDTH, CONV_CH)),
            _resident((1, CONV_CH)),
            _resident((1, CONV_CH)),
            _resident((1, CONV_CH)),
        ],
        out_specs=pl.BlockSpec((seq, CONV_CH), lambda b: (b, 0)),
        out_shape=jax.ShapeDtypeStruct((t, CONV_CH), BF16),
        scratch_shapes=[pltpu.VMEM((seq + 2 * CONV_HALO, CONV_CH), BF16),
                        pltpu.VMEM((tiles_per_step, rows, CONV_CH), F32)],
        compiler_params=_params(("parallel",)),
        name="conv_ln_swish",
    )(z, w_dw, b_dw, ln_g, ln_b)


def _merge_kernel(a_ref, c_ref, s_ref, gate_ref, x_ref, wa_ref, wc_ref, ws_ref, wo_ref, o_ref):
    merged = gate_ref[:, 0:D_MODEL].astype(F32) * _dot(a_ref[...], wa_ref[...])
    merged = merged + gate_ref[:, D_MODEL:2 * D_MODEL].astype(F32) * _dot(c_ref[...], wc_ref[...])
    merged = merged + gate_ref[:, 2 * D_MODEL:].astype(F32) * _dot(s_ref[...], ws_ref[...])
    o_ref[...] = x_ref[...] + _dot(merged.astype(BF16), wo_ref[...])


def _merge_call(attn, conv, sgu, gates, x2, wa, wc, ws, wo, tm=512):
    t = x2.shape[0]
    half = D_MODEL // 2
    return pl.pallas_call(
        _merge_kernel,
        grid=(t // tm,),
        in_specs=[
            pl.BlockSpec((tm, half), lambda i: (i, 0)),
            pl.BlockSpec((tm, half), lambda i: (i, 0)),
            pl.BlockSpec((tm, half), lambda i: (i, 0)),
            pl.BlockSpec((tm, N_BRANCH * D_MODEL), lambda i: (i, 0)),
            pl.BlockSpec((tm, D_MODEL), lambda i: (i, 0)),
            _resident((half, D_MODEL)),
            _resident((half, D_MODEL)),
            _resident((half, D_MODEL)),
            _resident((D_MODEL, D_MODEL)),
        ],
        out_specs=pl.BlockSpec((tm, D_MODEL), lambda i: (i, 0)),
        out_shape=jax.ShapeDtypeStruct((t, D_MODEL), F32),
        compiler_params=_params(("parallel",)),
        name="merge_out_proj",
    )(attn, conv, sgu, gates, x2, wa, wc, ws, wo)


def _ffn_kernel(x_ref, g_ref, wg_ref, wu_ref, wd_ref, gf_ref, o_ref, h_ref, *, final_norm):
    j = pl.program_id(1)

    @pl.when(j == 0)
    def _():
        x = x_ref[...]
        h_ref[...] = _rms(x, g_ref[...]).astype(BF16)
        o_ref[...] = x

    h = h_ref[...]
    a = _dot(h, wg_ref[...])
    b = _dot(h, wu_ref[...])
    mid = (a * jax.nn.sigmoid(a) * b).astype(BF16)
    o_ref[...] += _dot(mid, wd_ref[...])

    if final_norm:
        @pl.when(j == pl.num_programs(1) - 1)
        def _():
            o_ref[...] = _rms(o_ref[...], gf_ref[...])


def _ffn_call(x2, g, wg, wu, wd, g_final, final_norm, tm=1024, tf=512):
    t = x2.shape[0]
    d_ff = wg.shape[1]
    return pl.pallas_call(
        functools.partial(_ffn_kernel, final_norm=final_norm),
        grid=(t // tm, d_ff // tf),
        in_specs=[
            pl.BlockSpec((tm, D_MODEL), lambda i, j: (i, 0)),
            _resident((1, D_MODEL)),
            pl.BlockSpec((D_MODEL, tf), lambda i, j: (0, j)),
            pl.BlockSpec((D_MODEL, tf), lambda i, j: (0, j)),
            pl.BlockSpec((tf, D_MODEL), lambda i, j: (j, 0)),
            _resident((1, D_MODEL)),
        ],
        out_specs=pl.BlockSpec((tm, D_MODEL), lambda i, j: (i, 0)),
        out_shape=jax.ShapeDtypeStruct((t, D_MODEL), F32),
        scratch_shapes=[pltpu.VMEM((tm, D_MODEL), BF16)],
        compiler_params=_params(("parallel", "arbitrary")),
        name="swiglu_ffn",
    )(x2, g, wg, wu, wd, g_final)


def _rope_tables(seq):
    pos = jnp.arange(seq, dtype=jnp.int32)
    row = (pos // GRID_W).astype(F32)
    col = (pos % GRID_W).astype(F32)
    inv = ROPE_THETA ** (-jnp.arange(ROPE_FREQ_PER_AXIS, dtype=F32) / ROPE_FREQ_PER_AXIS)
    ang = jnp.concatenate([row[:, None] * inv, col[:, None] * inv], axis=-1)
    cos, sin = jnp.cos(ang), jnp.sin(ang)
    return jnp.concatenate([cos, cos], axis=-1), jnp.concatenate([-sin, sin], axis=-1)


def _gained_tables(cos2, sin2, gain, scale):
    return cos2 * (gain * scale), sin2 * (jnp.roll(gain, ROPE_PAIRS) * scale)


def kernel(x, g_mix, w_in, b_gate, q_norm_g, k_norm_g, w_attn_o, w_dw, b_dw, conv_ln_g, conv_ln_b, w_conv_o, sg_ln_g, sg_ln_b, w_s, b_s, w_sg_o, w_out, g_ffn, w_ff_gate, w_ff_up, w_ff_down, g_final):
    batch, seq, _ = x.shape
    depth = w_in.shape[0]
    cos2, sin2 = _rope_tables(seq)
    x2 = x.reshape(batch * seq, D_MODEL)
    row = lambda p: p.reshape(1, -1)
    w_in_bf16 = _cast_call(w_in, 0)
    for l in range(depth):
        h, qkv = _qkv_call(x2, row(g_mix[l]), w_in_bf16)
        z = _glu_call(h, w_in_bf16)
        sgu = _sgu_call(h, w_in_bf16, row(sg_ln_g[l]), row(sg_ln_b[l]), w_s[l], b_s[l].T)
        gates = _gates_call(h, w_in_bf16, row(b_gate[l]))
        tables = (*_gained_tables(cos2, sin2, q_norm_g[l], HEAD_DIM ** -0.5 * LOG2_E),
                  *_gained_tables(cos2, sin2, k_norm_g[l], 1.0))
        ride = [(w, l) for w in (w_attn_o, w_conv_o, w_sg_o, w_out, w_ff_gate, w_ff_up, w_ff_down)]
        if l + 1 < depth:
            ride.append((w_in, l + 1))
        attn, staged = _attn_call(qkv, tables, ride, batch, seq)
        w_a, w_c, w_s_o, w_o, w_fg, w_fu, w_fd = staged[:7]
        conv = _conv_call(z, w_dw[l].reshape(CONV_WIDTH, CONV_CH), row(b_dw[l]),
                          row(conv_ln_g[l]), row(conv_ln_b[l]), batch, seq)
        x2 = _merge_call(attn, conv, sgu, gates, x2, w_a, w_c, w_s_o, w_o)
        x2 = _ffn_call(x2, row(g_ffn[l]), w_fg, w_fu, w_fd, row(g_final), final_norm=(l == depth - 1))
        if l + 1 < depth:
            w_in_bf16 = staged[7]
    return x2.reshape(batch, seq, D_MODEL)
```

```python
import functools

import jax
import jax.numpy as jnp
from jax import lax
from jax.experimental import pallas as pl
from jax.experimental.pallas import tpu as pltpu

D_MODEL = 2048
GRID_W = 64
HEAD_DIM = 128
N_Q_HEADS = 8
N_KV_HEADS = 2
Q_PER_KV = N_Q_HEADS // N_KV_HEADS
ROPE_THETA = 10000.0
ROPE_PAIRS = HEAD_DIM // 2
ROPE_FREQ_PER_AXIS = ROPE_PAIRS // 2
CONV_CH = 1024
CONV_WIDTH = 31
CONV_PAD = CONV_WIDTH // 2
SG_CH = 1024
SG_GROUP_CH = 128
SG_GROUPS = SG_CH // SG_GROUP_CH
SG_CHUNK = 128
N_BRANCH = 3
Q_COLS = N_Q_HEADS * HEAD_DIM
KV_COLS = N_KV_HEADS * HEAD_DIM
QKV_COLS = Q_COLS + 2 * KV_COLS
RMS_EPS = 1e-6
LN_EPS = 1e-5
LOG2_E = 1.4426950408889634
IN_COLS = QKV_COLS + 2 * CONV_CH + 2 * SG_CH + N_BRANCH * D_MODEL
W_IN_COL_BLOCK = 512

V7X_VMEM_BYTES = 64 * 1024 * 1024
V7X_SUBLANES = 8
V7X_LANES = 128
BF16_SUBLANES = 16
CONV_HALO = -(-CONV_PAD // V7X_SUBLANES) * V7X_SUBLANES

BF16 = jnp.bfloat16
F32 = jnp.float32


def _params(semantics):
    return pltpu.CompilerParams(dimension_semantics=semantics, vmem_limit_bytes=V7X_VMEM_BYTES)


def _resident(shape):
    return pl.BlockSpec(shape, lambda *_: (0,) * len(shape), pipeline_mode=pl.Buffered(1))


def _w_in_cols(first_block, n_blocks):
    return [pl.BlockSpec((D_MODEL, W_IN_COL_BLOCK), functools.partial(lambda j, *_: (0, j), first_block + k),
                         pipeline_mode=pl.Buffered(1)) for k in range(n_blocks)]


def _rms(xf, gain):
    return xf * lax.rsqrt(jnp.mean(xf * xf, axis=-1, keepdims=True) + RMS_EPS) * gain


def _layer_norm(xf, gain, bias):
    mu = jnp.mean(xf, axis=-1, keepdims=True)
    xc = xf - mu
    return xc * lax.rsqrt(jnp.mean(xc * xc, axis=-1, keepdims=True) + LN_EPS) * gain + bias


def _sigmoid(x):
    return 0.5 * jnp.tanh(0.5 * x) + 0.5


def _silu(x):
    h = 0.5 * x
    return h + h * jnp.tanh(h)


def _dot(a, b):
    return jnp.dot(a, b, preferred_element_type=F32)


CAST_COL_BLOCK = 512
CAST_MAX_ROW_BLOCK = 2048


def _cast_kernel(w_ref, o_ref):
    o_ref[...] = w_ref[...].astype(BF16)


def _cast_call(w, layer):
    _, rows, cols = w.shape
    assert cols % CAST_COL_BLOCK == 0
    row_blk = next(r for r in range(min(rows, CAST_MAX_ROW_BLOCK), 0, -V7X_SUBLANES) if rows % r == 0)
    return pl.pallas_call(
        _cast_kernel,
        grid=(rows // row_blk, cols // CAST_COL_BLOCK),
        in_specs=[pl.BlockSpec((None, row_blk, CAST_COL_BLOCK), lambda i, j: (layer, i, j))],
        out_specs=pl.BlockSpec((row_blk, CAST_COL_BLOCK), lambda i, j: (i, j)),
        out_shape=jax.ShapeDtypeStruct((rows, cols), BF16),
        compiler_params=_params(("parallel", "parallel")),
        name="weight_to_bf16",
    )(w)


def _qkv_kernel(x_ref, g_ref, w_ref, h_ref, qkv_ref):
    h = _rms(x_ref[...], g_ref[...]).astype(BF16)
    h_ref[...] = h
    qkv_ref[...] = _dot(h, w_ref[...]).astype(BF16)


def _qkv_call(x2, g, w, tm=1024):
    t = x2.shape[0]
    return pl.pallas_call(
        _qkv_kernel,
        grid=(t // tm,),
        in_specs=[
            pl.BlockSpec((tm, D_MODEL), lambda i: (i, 0)),
            _resident((1, D_MODEL)),
            pl.BlockSpec((D_MODEL, QKV_COLS), lambda i: (0, 0), pipeline_mode=pl.Buffered(1)),
        ],
        out_specs=[
            pl.BlockSpec((tm, D_MODEL), lambda i: (i, 0)),
            pl.BlockSpec((tm, QKV_COLS), lambda i: (i, 0)),
        ],
        out_shape=[
            jax.ShapeDtypeStruct((t, D_MODEL), BF16),
            jax.ShapeDtypeStruct((t, QKV_COLS), BF16),
        ],
        compiler_params=_params(("parallel",)),
        name="rms_qkv",
    )(x2, g, w)


def _glu_kernel(h_ref, *refs):
    *w_refs, z_ref = refs
    n = len(w_refs) // 2
    h = h_ref[...]
    for k in range(n):
        cols = slice(k * W_IN_COL_BLOCK, (k + 1) * W_IN_COL_BLOCK)
        z_ref[:, cols] = (_dot(h, w_refs[k][...]) * _sigmoid(_dot(h, w_refs[n + k][...]))).astype(BF16)


def _glu_call(h, w_in_bf16, tm=1024):
    t = h.shape[0]
    first = QKV_COLS // W_IN_COL_BLOCK
    n_blocks = 2 * CONV_CH // W_IN_COL_BLOCK
    return pl.pallas_call(
        _glu_kernel,
        grid=(t // tm,),
        in_specs=[pl.BlockSpec((tm, D_MODEL), lambda i: (i, 0)), *_w_in_cols(first, n_blocks)],
        out_specs=pl.BlockSpec((tm, CONV_CH), lambda i: (i, 0)),
        out_shape=jax.ShapeDtypeStruct((t, CONV_CH), BF16),
        compiler_params=_params(("parallel",)),
        name="conv_glu_in",
    )(h, *([w_in_bf16] * n_blocks))


def _sgu_kernel(h_ref, *refs):
    *w_refs, g_ref, b_ref, ws_ref, bst_ref, o_ref = refs
    n = len(w_refs) // 2
    h = h_ref[...]
    project = lambda blocks: jax.nn.gelu(jnp.concatenate([_dot(h, w[...]) for w in blocks], axis=1))
    u = project(w_refs[:n])
    v = project(w_refs[n:])
    for grp in range(SG_GROUPS):
        cols = slice(grp * SG_GROUP_CH, (grp + 1) * SG_GROUP_CH)
        vn = _layer_norm(v[:, cols], g_ref[:, cols], b_ref[:, cols]).astype(BF16)
        w = ws_ref[grp].astype(BF16)
        bias = bst_ref[:, grp:grp + 1]
        for r0 in range(0, h.shape[0], SG_CHUNK):
            rows = slice(r0, r0 + SG_CHUNK)
            o_ref[rows, cols] = (u[rows, cols] * (_dot(w, vn[rows, :]) + bias)).astype(BF16)


def _sgu_call(h, w_in_bf16, ln_g, ln_b, ws, bst, tm=1024):
    t = h.shape[0]
    assert tm % SG_CHUNK == 0
    first = (QKV_COLS + 2 * CONV_CH) // W_IN_COL_BLOCK
    n_blocks = 2 * SG_CH // W_IN_COL_BLOCK
    return pl.pallas_call(
        _sgu_kernel,
        grid=(t // tm,),
        in_specs=[
            pl.BlockSpec((tm, D_MODEL), lambda i: (i, 0)),
            *_w_in_cols(first, n_blocks),
            _resident((1, SG_CH)),
            _resident((1, SG_CH)),
            _resident((SG_GROUPS, SG_CHUNK, SG_CHUNK)),
            _resident((SG_CHUNK, SG_GROUPS)),
        ],
        out_specs=pl.BlockSpec((tm, SG_CH), lambda i: (i, 0)),
        out_shape=jax.ShapeDtypeStruct((t, SG_CH), BF16),
        compiler_params=_params(("parallel",)),
        name="spatial_gating",
    )(h, *([w_in_bf16] * n_blocks), ln_g, ln_b, ws, bst)


def _gates_kernel(h_ref, *refs):
    *w_refs, b_ref, o_ref = refs
    h = h_ref[...]
    for k, w_ref in enumerate(w_refs):
        cols = slice(k * W_IN_COL_BLOCK, (k + 1) * W_IN_COL_BLOCK)
        o_ref[:, cols] = _sigmoid(_dot(h, w_ref[...]) + b_ref[:, cols]).astype(BF16)


def _gates_call(h, w_in_bf16, b, tm=1024):
    t = h.shape[0]
    n = N_BRANCH * D_MODEL
    first = (IN_COLS - n) // W_IN_COL_BLOCK
    n_blocks = n // W_IN_COL_BLOCK
    return pl.pallas_call(
        _gates_kernel,
        grid=(t // tm,),
        in_specs=[pl.BlockSpec((tm, D_MODEL), lambda i: (i, 0)), *_w_in_cols(first, n_blocks),
                  _resident((1, n))],
        out_specs=pl.BlockSpec((tm, n), lambda i: (i, 0)),
        out_shape=jax.ShapeDtypeStruct((t, n), BF16),
        compiler_params=_params(("parallel",)),
        name="branch_gates",
    )(h, *([w_in_bf16] * n_blocks), b)


def _norm_rope(y, cos, sin):
    y = y.astype(F32)
    rinv = lax.rsqrt(jnp.mean(y * y, axis=-1, keepdims=True) + RMS_EPS)
    return ((y * cos + pltpu.roll(y, ROPE_PAIRS, axis=1) * sin) * rinv).astype(BF16)


def _attn_kernel(q_ref, k_ref, v_ref, qcos_ref, qsin_ref, kcos_ref, ksin_ref, *rest, sub_rows, n_ride):
    ride_in, (o_ref, *ride_out), k_scr = rest[:n_ride], rest[n_ride:-1], rest[-1]
    for src, dst in zip(ride_in, ride_out):
        dst[...] = src[...].astype(BF16)

    @pl.when(pl.program_id(2) == 0)
    def _():
        k_scr[...] = _norm_rope(k_ref[...], kcos_ref[...], ksin_ref[...])

    k = k_scr[...]
    v = v_ref[...]
    v_ext = jnp.concatenate([v, jnp.ones_like(v)], axis=1)
    tq = q_ref.shape[0]
    for r0 in range(0, tq, sub_rows):
        rows = slice(r0, r0 + sub_rows)
        for grp in range(Q_PER_KV):
            lo = grp * HEAD_DIM
            q = _norm_rope(q_ref[rows, lo:lo + HEAD_DIM], qcos_ref[rows, :], qsin_ref[rows, :])
            s = lax.dot_general(q, k, (((1,), (1,)), ((), ())), preferred_element_type=F32)
            p = jnp.exp2(s - jnp.max(s, axis=-1, keepdims=True)).astype(BF16)
            o = _dot(p, v_ext)
            o_ref[rows, lo:lo + HEAD_DIM] = (o[:, :HEAD_DIM] / o[:, HEAD_DIM:]).astype(BF16)


def _attn_call(qkv, tables, ride, batch, seq, tq=1024):
    t = qkv.shape[0]
    q_blocks = seq // tq
    grp_cols = Q_PER_KV * HEAD_DIM
    k_col0 = Q_COLS // HEAD_DIM
    v_col0 = (Q_COLS + KV_COLS) // HEAD_DIM
    q_table = pl.BlockSpec((tq, HEAD_DIM), lambda b, h, i: (i, 0))
    k_table = pl.BlockSpec((seq, HEAD_DIM), lambda b, h, i: (0, 0))
    n_steps = batch * N_KV_HEADS * q_blocks
    step = lambda b, h, i: (b * N_KV_HEADS + h) * q_blocks + i
    ride_in, ride_out, ride_shapes = [], [], []
    for w, layer in ride:
        _, rows, cols = w.shape
        slab = rows // n_steps
        assert slab * n_steps == rows and slab % BF16_SUBLANES == 0
        ride_in.append(pl.BlockSpec((None, slab, cols), functools.partial(
            lambda layer, b, h, i: (layer, step(b, h, i), 0), layer)))
        ride_out.append(pl.BlockSpec((slab, cols), lambda b, h, i: (step(b, h, i), 0)))
        ride_shapes.append(jax.ShapeDtypeStruct((rows, cols), BF16))
    out = pl.pallas_call(
        functools.partial(_attn_kernel, sub_rows=256, n_ride=len(ride)),
        grid=(batch, N_KV_HEADS, q_blocks),
        in_specs=[
            pl.BlockSpec((tq, grp_cols), lambda b, h, i: (b * q_blocks + i, h)),
            pl.BlockSpec((seq, HEAD_DIM), lambda b, h, i: (b, k_col0 + h)),
            pl.BlockSpec((seq, HEAD_DIM), lambda b, h, i: (b, v_col0 + h)),
            q_table, q_table, k_table, k_table, *ride_in,
        ],
        out_specs=[pl.BlockSpec((tq, grp_cols), lambda b, h, i: (b * q_blocks + i, h)), *ride_out],
        out_shape=[jax.ShapeDtypeStruct((t, Q_COLS), BF16), *ride_shapes],
        scratch_shapes=[pltpu.VMEM((seq, HEAD_DIM), BF16)],
        compiler_params=_params(("arbitrary", "arbitrary", "arbitrary")),
        name="gqa_attention",
    )(qkv, qkv, qkv, *tables, *(w for w, _ in ride))
    return out[0], out[1:]


def _conv_kernel(z_ref, w_ref, b_ref, g_ref, beta_ref, o_ref, pad_ref, conv_ref, *, seq, rows,
                 tiles_per_step):
    zeros = jnp.zeros((CONV_HALO, CONV_CH), BF16)
    pad_ref[0:CONV_HALO, :] = zeros
    pad_ref[CONV_HALO + seq:, :] = zeros
    pad_ref[CONV_HALO:CONV_HALO + seq, :] = z_ref[...]
    win_rows = rows + 2 * CONV_HALO
    lane_blocks = CONV_CH // V7X_LANES
    row_groups = 2 * CONV_HALO // V7X_SUBLANES
    n_shifts = V7X_SUBLANES - 1
    r = lax.broadcasted_iota(jnp.int32, (n_shifts * win_rows, win_rows), 0)
    c = lax.broadcasted_iota(jnp.int32, (n_shifts * win_rows, win_rows), 1)
    shift_mat = (c == r % win_rows + r // win_rows + 1).astype(BF16)

    half_g = 0.5 * g_ref[...]
    half_beta = 0.5 * beta_ref[...]

    def conv_tile(r0, tile_ref):
        win = pad_ref[pl.ds(r0, win_rows), :]
        shifted_all = _dot(shift_mat, win)
        win_f32 = win.astype(F32)
        for cb in range(lane_blocks):
            lanes = slice(cb * V7X_LANES, (cb + 1) * V7X_LANES)
            acc = jnp.zeros((rows, V7X_LANES), F32)
            for sub in range(V7X_SUBLANES):
                base = (sub - 1) * win_rows
                for grp in range(row_groups):
                    tap = V7X_SUBLANES * grp + sub - (CONV_HALO - CONV_PAD)
                    if 0 <= tap < CONV_WIDTH:
                        lo = V7X_SUBLANES * grp
                        src = (win_f32[lo:lo + rows, lanes] if sub == 0
                               else shifted_all[base + lo:base + lo + rows, lanes])
                        acc = acc + src * w_ref[tap:tap + 1, lanes]
            tile_ref[:, lanes] = acc + b_ref[:, lanes]
        h = _layer_norm(tile_ref[...], half_g, half_beta)
        o_ref[pl.ds(r0, rows), :] = (h + h * jnp.tanh(h)).astype(BF16)

    def step(i, carry):
        for u in range(tiles_per_step):
            conv_tile(pl.multiple_of((i * tiles_per_step + u) * rows, rows), conv_ref.at[u])
        return carry

    lax.fori_loop(0, seq // (rows * tiles_per_step), step, 0)


def _conv_call(z, w_dw, b_dw, ln_g, ln_b, batch, seq, rows=64, tiles_per_step=4):
    t = z.shape[0]
    return pl.pallas_call(
        functools.partial(_conv_kernel, seq=seq, rows=rows, tiles_per_step=tiles_per_step),
        grid=(batch,),
        in_specs=[
            pl.BlockSpec((seq, CONV_CH), lambda b: (b, 0)),
            _resident((CONV_WIDTH, CONV_CH)),
            _resident((1, CONV_CH)),
            _resident((1, CONV_CH)),
            _resident((1, CONV_CH)),
        ],
        out_specs=pl.BlockSpec((seq, CONV_CH), lambda b: (b, 0)),
        out_shape=jax.ShapeDtypeStruct((t, CONV_CH), BF16),
        scratch_shapes=[pltpu.VMEM((seq + 2 * CONV_HALO, CONV_CH), BF16),
                        pltpu.VMEM((tiles_per_step, rows, CONV_CH), F32)],
        compiler_params=_params(("parallel",)),
        name="conv_ln_swish",
    )(z, w_dw, b_dw, ln_g, ln_b)


def _merge_kernel(a_ref, c_ref, s_ref, gate_ref, x_ref, wa_ref, wc_ref, ws_ref, wo_ref, o_ref):
    merged = gate_ref[:, 0:D_MODEL].astype(F32) * _dot(a_ref[...], wa_ref[...])
    merged = merged + gate_ref[:, D_MODEL:2 * D_MODEL].astype(F32) * _dot(c_ref[...], wc_ref[...])
    merged = merged + gate_ref[:, 2 * D_MODEL:].astype(F32) * _dot(s_ref[...], ws_ref[...])
    o_ref[...] = x_ref[...] + _dot(merged.astype(BF16), wo_ref[...])


def _merge_call(attn, conv, sgu, gates, x2, wa, wc, ws, wo, tm=512):
    t = x2.shape[0]
    half = D_MODEL // 2
    return pl.pallas_call(
        _merge_kernel,
        grid=(t // tm,),
        in_specs=[
            pl.BlockSpec((tm, half), lambda i: (i, 0)),
            pl.BlockSpec((tm, half), lambda i: (i, 0)),
            pl.BlockSpec((tm, half), lambda i: (i, 0)),
            pl.BlockSpec((tm, N_BRANCH * D_MODEL), lambda i: (i, 0)),
            pl.BlockSpec((tm, D_MODEL), lambda i: (i, 0)),
            _resident((half, D_MODEL)),
            _resident((half, D_MODEL)),
            _resident((half, D_MODEL)),
            _resident((D_MODEL, D_MODEL)),
        ],
        out_specs=pl.BlockSpec((tm, D_MODEL), lambda i: (i, 0)),
        out_shape=jax.ShapeDtypeStruct((t, D_MODEL), F32),
        compiler_params=_params(("parallel",)),
        name="merge_out_proj",
    )(attn, conv, sgu, gates, x2, wa, wc, ws, wo)


def _ffn_kernel(x_ref, g_ref, wg_ref, wu_ref, wd_ref, gf_ref, o_ref, h_ref, *, final_norm):
    j = pl.program_id(1)

    @pl.when(j == 0)
    def _():
        x = x_ref[...]
        h_ref[...] = _rms(x, g_ref[...]).astype(BF16)
        o_ref[...] = x

    h = h_ref[...]
    a = _dot(h, wg_ref[...])
    b = _dot(h, wu_ref[...])
    mid = (_silu(a) * b).astype(BF16)
    o_ref[...] += _dot(mid, wd_ref[...])

    if final_norm:
        @pl.when(j == pl.num_programs(1) - 1)
        def _():
            o_ref[...] = _rms(o_ref[...], gf_ref[...])


def _ffn_call(x2, g, wg, wu, wd, g_final, final_norm, tm=1024, tf=512):
    t = x2.shape[0]
    d_ff = wg.shape[1]
    return pl.pallas_call(
        functools.partial(_ffn_kernel, final_norm=final_norm),
        grid=(t // tm, d_ff // tf),
        in_specs=[
            pl.BlockSpec((tm, D_MODEL), lambda i, j: (i, 0)),
            _resident((1, D_MODEL)),
            pl.BlockSpec((D_MODEL, tf), lambda i, j: (0, j)),
            pl.BlockSpec((D_MODEL, tf), lambda i, j: (0, j)),
            pl.BlockSpec((tf, D_MODEL), lambda i, j: (j, 0)),
            _resident((1, D_MODEL)),
        ],
        out_specs=pl.BlockSpec((tm, D_MODEL), lambda i, j: (i, 0)),
        out_shape=jax.ShapeDtypeStruct((t, D_MODEL), F32),
        scratch_shapes=[pltpu.VMEM((tm, D_MODEL), BF16)],
        compiler_params=_params(("parallel", "arbitrary")),
        name="swiglu_ffn",
    )(x2, g, wg, wu, wd, g_final)


def _rope_tables(seq):
    pos = jnp.arange(seq, dtype=jnp.int32)
    row = (pos // GRID_W).astype(F32)
    col = (pos % GRID_W).astype(F32)
    inv = ROPE_THETA ** (-jnp.arange(ROPE_FREQ_PER_AXIS, dtype=F32) / ROPE_FREQ_PER_AXIS)
    ang = jnp.concatenate([row[:, None] * inv, col[:, None] * inv], axis=-1)
    cos, sin = jnp.cos(ang), jnp.sin(ang)
    return jnp.concatenate([cos, cos], axis=-1), jnp.concatenate([-sin, sin], axis=-1)


def _gained_tables(cos2, sin2, gain, scale):
    return cos2 * (gain * scale), sin2 * (jnp.roll(gain, ROPE_PAIRS) * scale)


def kernel(x, g_mix, w_in, b_gate, q_norm_g, k_norm_g, w_attn_o, w_dw, b_dw, conv_ln_g, conv_ln_b, w_conv_o, sg_ln_g, sg_ln_b, w_s, b_s, w_sg_o, w_out, g_ffn, w_ff_gate, w_ff_up, w_ff_down, g_final):
    batch, seq, _ = x.shape
    depth = w_in.shape[0]
    cos2, sin2 = _rope_tables(seq)
    x2 = x.reshape(batch * seq, D_MODEL)
    row = lambda p: p.reshape(1, -1)
    w_in_bf16 = _cast_call(w_in, 0)
    for l in range(depth):
        h, qkv = _qkv_call(x2, row(g_mix[l]), w_in_bf16)
        z = _glu_call(h, w_in_bf16)
        sgu = _sgu_call(h, w_in_bf16, row(sg_ln_g[l]), row(sg_ln_b[l]), w_s[l], b_s[l].T)
        gates = _gates_call(h, w_in_bf16, row(b_gate[l]))
        tables = (*_gained_tables(cos2, sin2, q_norm_g[l], HEAD_DIM ** -0.5 * LOG2_E),
                  *_gained_tables(cos2, sin2, k_norm_g[l], 1.0))
        ride = [(w, l) for w in (w_attn_o, w_conv_o, w_sg_o, w_out, w_ff_gate, w_ff_up, w_ff_down)]
        if l + 1 < depth:
            ride.append((w_in, l + 1))
        attn, staged = _attn_call(qkv, tables, ride, batch, seq)
        w_a, w_c, w_s_o, w_o, w_fg, w_fu, w_fd = staged[:7]
        conv = _conv_call(z, w_dw[l].reshape(CONV_WIDTH, CONV_CH), row(b_dw[l]),
                          row(conv_ln_g[l]), row(conv_ln_b[l]), batch, seq)
        x2 = _merge_call(attn, conv, sgu, gates, x2, w_a, w_c, w_s_o, w_o)
        x2 = _ffn_call(x2, row(g_ffn[l]), w_fg, w_fu, w_fd, row(g_final), final_norm=(l == depth - 1))
        if l + 1 < depth:
            w_in_bf16 = staged[7]
    return x2.reshape(batch, seq, D_MODEL)
```

```python
import functools

import jax
import jax.numpy as jnp
from jax import lax
from jax.experimental import pallas as pl
from jax.experimental.pallas import tpu as pltpu

D_MODEL = 2048
GRID_W = 64
HEAD_DIM = 128
N_Q_HEADS = 8
N_KV_HEADS = 2
Q_PER_KV = N_Q_HEADS // N_KV_HEADS
ROPE_THETA = 10000.0
ROPE_PAIRS = HEAD_DIM // 2
ROPE_FREQ_PER_AXIS = ROPE_PAIRS // 2
CONV_CH = 1024
CONV_WIDTH = 31
CONV_PAD = CONV_WIDTH // 2
SG_CH = 1024
SG_GROUP_CH = 128
SG_GROUPS = SG_CH // SG_GROUP_CH
SG_CHUNK = 128
N_BRANCH = 3
Q_COLS = N_Q_HEADS * HEAD_DIM
KV_COLS = N_KV_HEADS * HEAD_DIM
QKV_COLS = Q_COLS + 2 * KV_COLS
RMS_EPS = 1e-6
LN_EPS = 1e-5
LOG2_E = 1.4426950408889634
IN_COLS = QKV_COLS + 2 * CONV_CH + 2 * SG_CH + N_BRANCH * D_MODEL
W_IN_COL_BLOCK = 512

V7X_VMEM_BYTES = 64 * 1024 * 1024
V7X_SUBLANES = 8
V7X_LANES = 128
BF16_SUBLANES = 16
CONV_HALO = -(-CONV_PAD // V7X_SUBLANES) * V7X_SUBLANES

BF16 = jnp.bfloat16
F32 = jnp.float32


def _params(semantics):
    return pltpu.CompilerParams(dimension_semantics=semantics, vmem_limit_bytes=V7X_VMEM_BYTES)


def _resident(shape):
    return pl.BlockSpec(shape, lambda *_: (0,) * len(shape), pipeline_mode=pl.Buffered(1))


def _w_in_cols(first_block, n_blocks):
    return [pl.BlockSpec((D_MODEL, W_IN_COL_BLOCK), functools.partial(lambda j, *_: (0, j), first_block + k),
                         pipeline_mode=pl.Buffered(1)) for k in range(n_blocks)]


def _rms(xf, gain):
    return xf * lax.rsqrt(jnp.mean(xf * xf, axis=-1, keepdims=True) + RMS_EPS) * gain


def _layer_norm(xf, gain, bias):
    mu = jnp.mean(xf, axis=-1, keepdims=True)
    xc = xf - mu
    return xc * lax.rsqrt(jnp.mean(xc * xc, axis=-1, keepdims=True) + LN_EPS) * gain + bias


def _sigmoid(x):
    return 0.5 * jnp.tanh(0.5 * x) + 0.5


def _silu(x):
    h = 0.5 * x
    return h + h * jnp.tanh(h)


def _dot(a, b):
    return jnp.dot(a, b, preferred_element_type=F32)


CAST_COL_BLOCK = 512
CAST_MAX_ROW_BLOCK = 2048


def _cast_kernel(w_ref, o_ref):
    o_ref[...] = w_ref[...].astype(BF16)


def _cast_call(w, layer):
    _, rows, cols = w.shape
    assert cols % CAST_COL_BLOCK == 0
    row_blk = next(r for r in range(min(rows, CAST_MAX_ROW_BLOCK), 0, -V7X_SUBLANES) if rows % r == 0)
    return pl.pallas_call(
        _cast_kernel,
        grid=(rows // row_blk, cols // CAST_COL_BLOCK),
        in_specs=[pl.BlockSpec((None, row_blk, CAST_COL_BLOCK), lambda i, j: (layer, i, j))],
        out_specs=pl.BlockSpec((row_blk, CAST_COL_BLOCK), lambda i, j: (i, j)),
        out_shape=jax.ShapeDtypeStruct((rows, cols), BF16),
        compiler_params=_params(("parallel", "parallel")),
        name="weight_to_bf16",
    )(w)


def _qkv_kernel(x_ref, g_ref, w_ref, h_ref, qkv_ref):
    h = _rms(x_ref[...], g_ref[...]).astype(BF16)
    h_ref[...] = h
    qkv_ref[...] = _dot(h, w_ref[...]).astype(BF16)


def _qkv_call(x2, g, w, tm=1024):
    t = x2.shape[0]
    return pl.pallas_call(
        _qkv_kernel,
        grid=(t // tm,),
        in_specs=[
            pl.BlockSpec((tm, D_MODEL), lambda i: (i, 0)),
            _resident((1, D_MODEL)),
            pl.BlockSpec((D_MODEL, QKV_COLS), lambda i: (0, 0), pipeline_mode=pl.Buffered(1)),
        ],
        out_specs=[
            pl.BlockSpec((tm, D_MODEL), lambda i: (i, 0)),
            pl.BlockSpec((tm, QKV_COLS), lambda i: (i, 0)),
        ],
        out_shape=[
            jax.ShapeDtypeStruct((t, D_MODEL), BF16),
            jax.ShapeDtypeStruct((t, QKV_COLS), BF16),
        ],
        compiler_params=_params(("parallel",)),
        name="rms_qkv",
    )(x2, g, w)


GLU_COL_STEP = 256


def _glu_kernel(h_ref, *refs):
    *w_refs, z_ref = refs
    n = len(w_refs) // 2
    h = h_ref[...]
    for k in range(n):
        for c0 in range(0, W_IN_COL_BLOCK, GLU_COL_STEP):
            loc = slice(c0, c0 + GLU_COL_STEP)
            cols = slice(k * W_IN_COL_BLOCK + c0, k * W_IN_COL_BLOCK + c0 + GLU_COL_STEP)
            z_ref[:, cols] = (_dot(h, w_refs[k][:, loc]) * _sigmoid(_dot(h, w_refs[n + k][:, loc]))).astype(BF16)


def _glu_call(h, w_in_bf16, tm=1024):
    t = h.shape[0]
    first = QKV_COLS // W_IN_COL_BLOCK
    n_blocks = 2 * CONV_CH // W_IN_COL_BLOCK
    return pl.pallas_call(
        _glu_kernel,
        grid=(t // tm,),
        in_specs=[pl.BlockSpec((tm, D_MODEL), lambda i: (i, 0)), *_w_in_cols(first, n_blocks)],
        out_specs=pl.BlockSpec((tm, CONV_CH), lambda i: (i, 0)),
        out_shape=jax.ShapeDtypeStruct((t, CONV_CH), BF16),
        compiler_params=_params(("parallel",)),
        name="conv_glu_in",
    )(h, *([w_in_bf16] * n_blocks))


SGU_COL_STEP = 256


def _sgu_kernel(h_ref, *refs):
    *w_refs, g_ref, b_ref, ws_ref, bst_ref, o_ref, vn_ref = refs
    n = len(w_refs) // 2
    h = h_ref[...]

    def stages(blocks):
        for k, w_ref in enumerate(blocks):
            for c0 in range(0, W_IN_COL_BLOCK, SGU_COL_STEP):
                yield k * W_IN_COL_BLOCK + c0, jax.nn.gelu(_dot(h, w_ref[:, c0:c0 + SGU_COL_STEP]))

    for col0, v in stages(w_refs[n:]):
        for lo in range(0, SGU_COL_STEP, SG_GROUP_CH):
            cols = slice(col0 + lo, col0 + lo + SG_GROUP_CH)
            vn_ref[:, cols] = _layer_norm(v[:, lo:lo + SG_GROUP_CH], g_ref[:, cols], b_ref[:, cols]).astype(BF16)
    for col0, u in stages(w_refs[:n]):
        for lo in range(0, SGU_COL_STEP, SG_GROUP_CH):
            grp = (col0 + lo) // SG_GROUP_CH
            cols = slice(col0 + lo, col0 + lo + SG_GROUP_CH)
            w = ws_ref[grp].astype(BF16)
            bias = bst_ref[:, grp:grp + 1]
            for r0 in range(0, h.shape[0], SG_CHUNK):
                rows = slice(r0, r0 + SG_CHUNK)
                mixed = _dot(w, vn_ref[rows, cols]) + bias
                o_ref[rows, cols] = (u[rows, lo:lo + SG_GROUP_CH] * mixed).astype(BF16)


def _sgu_call(h, w_in_bf16, ln_g, ln_b, ws, bst, tm=1024):
    t = h.shape[0]
    assert tm % SG_CHUNK == 0
    first = (QKV_COLS + 2 * CONV_CH) // W_IN_COL_BLOCK
    n_blocks = 2 * SG_CH // W_IN_COL_BLOCK
    return pl.pallas_call(
        _sgu_kernel,
        grid=(t // tm,),
        in_specs=[
            pl.BlockSpec((tm, D_MODEL), lambda i: (i, 0)),
            *_w_in_cols(first, n_blocks),
            _resident((1, SG_CH)),
            _resident((1, SG_CH)),
            _resident((SG_GROUPS, SG_CHUNK, SG_CHUNK)),
            _resident((SG_CHUNK, SG_GROUPS)),
        ],
        out_specs=pl.BlockSpec((tm, SG_CH), lambda i: (i, 0)),
        out_shape=jax.ShapeDtypeStruct((t, SG_CH), BF16),
        scratch_shapes=[pltpu.VMEM((tm, SG_CH), BF16)],
        compiler_params=_params(("parallel",)),
        name="spatial_gating",
    )(h, *([w_in_bf16] * n_blocks), ln_g, ln_b, ws, bst)


def _gates_kernel(h_ref, *refs):
    *w_refs, b_ref, o_ref = refs
    h = h_ref[...]
    for k, w_ref in enumerate(w_refs):
        cols = slice(k * W_IN_COL_BLOCK, (k + 1) * W_IN_COL_BLOCK)
        o_ref[:, cols] = _sigmoid(_dot(h, w_ref[...]) + b_ref[:, cols]).astype(BF16)


def _gates_call(h, w_in_bf16, b, tm=1024):
    t = h.shape[0]
    n = N_BRANCH * D_MODEL
    first = (IN_COLS - n) // W_IN_COL_BLOCK
    n_blocks = n // W_IN_COL_BLOCK
    return pl.pallas_call(
        _gates_kernel,
        grid=(t // tm,),
        in_specs=[pl.BlockSpec((tm, D_MODEL), lambda i: (i, 0)), *_w_in_cols(first, n_blocks),
                  _resident((1, n))],
        out_specs=pl.BlockSpec((tm, n), lambda i: (i, 0)),
        out_shape=jax.ShapeDtypeStruct((t, n), BF16),
        compiler_params=_params(("parallel",)),
        name="branch_gates",
    )(h, *([w_in_bf16] * n_blocks), b)


def _norm_rope(y, cos, sin):
    y = y.astype(F32)
    rinv = lax.rsqrt(jnp.mean(y * y, axis=-1, keepdims=True) + RMS_EPS)
    return ((y * cos + pltpu.roll(y, ROPE_PAIRS, axis=1) * sin) * rinv).astype(BF16)


def _attn_kernel(q_ref, k_ref, v_ref, qcos_ref, qsin_ref, kcos_ref, ksin_ref, *rest, sub_rows, n_ride):
    ride_in, (o_ref, *ride_out), k_scr = rest[:n_ride], rest[n_ride:-1], rest[-1]
    for src, dst in zip(ride_in, ride_out):
        dst[...] = src[...].astype(BF16)

    @pl.when(pl.program_id(2) == 0)
    def _():
        k_scr[...] = _norm_rope(k_ref[...], kcos_ref[...], ksin_ref[...])

    k = k_scr[...]
    v = v_ref[...]
    v_ext = jnp.concatenate([v, jnp.ones_like(v)], axis=1)
    tq = q_ref.shape[0]
    for r0 in range(0, tq, sub_rows):
        rows = slice(r0, r0 + sub_rows)
        for grp in range(Q_PER_KV):
            lo = grp * HEAD_DIM
            q = _norm_rope(q_ref[rows, lo:lo + HEAD_DIM], qcos_ref[rows, :], qsin_ref[rows, :])
            s = lax.dot_general(q, k, (((1,), (1,)), ((), ())), preferred_element_type=F32)
            p = jnp.exp2(s - jnp.max(s, axis=-1, keepdims=True)).astype(BF16)
            o = _dot(p, v_ext)
            o_ref[rows, lo:lo + HEAD_DIM] = (o[:, :HEAD_DIM] / o[:, HEAD_DIM:]).astype(BF16)


def _attn_call(qkv, tables, ride, batch, seq, tq=1024):
    t = qkv.shape[0]
    q_blocks = seq // tq
    grp_cols = Q_PER_KV * HEAD_DIM
    k_col0 = Q_COLS // HEAD_DIM
    v_col0 = (Q_COLS + KV_COLS) // HEAD_DIM
    q_table = pl.BlockSpec((tq, HEAD_DIM), lambda b, h, i: (i, 0))
    k_table = pl.BlockSpec((seq, HEAD_DIM), lambda b, h, i: (0, 0))
    n_steps = batch * N_KV_HEADS * q_blocks
    step = lambda b, h, i: (b * N_KV_HEADS + h) * q_blocks + i
    ride_in, ride_out, ride_shapes = [], [], []
    for w, layer in ride:
        _, rows, cols = w.shape
        slab = rows // n_steps
        assert slab * n_steps == rows and slab % BF16_SUBLANES == 0
        ride_in.append(pl.BlockSpec((None, slab, cols), functools.partial(
            lambda layer, b, h, i: (layer, step(b, h, i), 0), layer)))
        ride_out.append(pl.BlockSpec((slab, cols), lambda b, h, i: (step(b, h, i), 0)))
        ride_shapes.append(jax.ShapeDtypeStruct((rows, cols), BF16))
    out = pl.pallas_call(
        functools.partial(_attn_kernel, sub_rows=256, n_ride=len(ride)),
        grid=(batch, N_KV_HEADS, q_blocks),
        in_specs=[
            pl.BlockSpec((tq, grp_cols), lambda b, h, i: (b * q_blocks + i, h)),
            pl.BlockSpec((seq, HEAD_DIM), lambda b, h, i: (b, k_col0 + h)),
            pl.BlockSpec((seq, HEAD_DIM), lambda b, h, i: (b, v_col0 + h)),
            q_table, q_table, k_table, k_table, *ride_in,
        ],
        out_specs=[pl.BlockSpec((tq, grp_cols), lambda b, h, i: (b * q_blocks + i, h)), *ride_out],
        out_shape=[jax.ShapeDtypeStruct((t, Q_COLS), BF16), *ride_shapes],
        scratch_shapes=[pltpu.VMEM((seq, HEAD_DIM), BF16)],
        compiler_params=_params(("arbitrary", "arbitrary", "arbitrary")),
        name="gqa_attention",
    )(qkv, qkv, qkv, *tables, *(w for w, _ in ride))
    return out[0], out[1:]


def _conv_kernel(z_ref, w_ref, b_ref, g_ref, beta_ref, o_ref, pad_ref, conv_ref, *, seq, rows,
                 tiles_per_step):
    zeros = jnp.zeros((CONV_HALO, CONV_CH), BF16)
    pad_ref[0:CONV_HALO, :] = zeros
    pad_ref[CONV_HALO + seq:, :] = zeros
    pad_ref[CONV_HALO:CONV_HALO + seq, :] = z_ref[...]
    win_rows = rows + 2 * CONV_HALO
    lane_blocks = CONV_CH // V7X_LANES
    row_groups = 2 * CONV_HALO // V7X_SUBLANES
    n_shifts = V7X_SUBLANES - 1
    r = lax.broadcasted_iota(jnp.int32, (n_shifts * win_rows, win_rows), 0)
    c = lax.broadcasted_iota(jnp.int32, (n_shifts * win_rows, win_rows), 1)
    shift_mat = (c == r % win_rows + r // win_rows + 1).astype(BF16)

    half_g = 0.5 * g_ref[...]
    half_beta = 0.5 * beta_ref[...]

    def conv_tile(r0, tile_ref):
        win = pad_ref[pl.ds(r0, win_rows), :]
        shifted_all = _dot(shift_mat, win)
        win_f32 = win.astype(F32)
        for cb in range(lane_blocks):
            lanes = slice(cb * V7X_LANES, (cb + 1) * V7X_LANES)
            acc = jnp.zeros((rows, V7X_LANES), F32)
            for sub in range(V7X_SUBLANES):
                base = (sub - 1) * win_rows
                for grp in range(row_groups):
                    tap = V7X_SUBLANES * grp + sub - (CONV_HALO - CONV_PAD)
                    if 0 <= tap < CONV_WIDTH:
                        lo = V7X_SUBLANES * grp
                        src = (win_f32[lo:lo + rows, lanes] if sub == 0
                               else shifted_all[base + lo:base + lo + rows, lanes])
                        acc = acc + src * w_ref[tap:tap + 1, lanes]
            tile_ref[:, lanes] = acc + b_ref[:, lanes]
        h = _layer_norm(tile_ref[...], half_g, half_beta)
        o_ref[pl.ds(r0, rows), :] = (h + h * jnp.tanh(h)).astype(BF16)

    def step(i, carry):
        for u in range(tiles_per_step):
            conv_tile(pl.multiple_of((i * tiles_per_step + u) * rows, rows), conv_ref.at[u])
        return carry

    lax.fori_loop(0, seq // (rows * tiles_per_step), step, 0)


def _conv_call(z, w_dw, b_dw, ln_g, ln_b, batch, seq, rows=64, tiles_per_step=4):
    t = z.shape[0]
    return pl.pallas_call(
        functools.partial(_conv_kernel, seq=seq, rows=rows, tiles_per_step=tiles_per_step),
        grid=(batch,),
        in_specs=[
            pl.BlockSpec((seq, CONV_CH), lambda b: (b, 0)),
            _resident((CONV_WIDTH, CONV_CH)),
            _resident((1, CONV_CH)),
            _resident((1, CONV_CH)),
            _resident((1, CONV_CH)),
        ],
        out_specs=pl.BlockSpec((seq, CONV_CH), lambda b: (b, 0)),
        out_shape=jax.ShapeDtypeStruct((t, CONV_CH), BF16),
        scratch_shapes=[pltpu.VMEM((seq + 2 * CONV_HALO, CONV_CH), BF16),
                        pltpu.VMEM((tiles_per_step, rows, CONV_CH), F32)],
        compiler_params=_params(("parallel",)),
        name="conv_ln_swish",
    )(z, w_dw, b_dw, ln_g, ln_b)


def _merge_kernel(a_ref, c_ref, s_ref, gate_ref, x_ref, wa_ref, wc_ref, ws_ref, wo_ref, o_ref):
    merged = gate_ref[:, 0:D_MODEL].astype(F32) * _dot(a_ref[...], wa_ref[...])
    merged = merged + gate_ref[:, D_MODEL:2 * D_MODEL].astype(F32) * _dot(c_ref[...], wc_ref[...])
    merged = merged + gate_ref[:, 2 * D_MODEL:].astype(F32) * _dot(s_ref[...], ws_ref[...])
    o_ref[...] = x_ref[...] + _dot(merged.astype(BF16), wo_ref[...])


def _merge_call(attn, conv, sgu, gates, x2, wa, wc, ws, wo, tm=512):
    t = x2.shape[0]
    half = D_MODEL // 2
    return pl.pallas_call(
        _merge_kernel,
        grid=(t // tm,),
        in_specs=[
            pl.BlockSpec((tm, half), lambda i: (i, 0)),
            pl.BlockSpec((tm, half), lambda i: (i, 0)),
            pl.BlockSpec((tm, half), lambda i: (i, 0)),
            pl.BlockSpec((tm, N_BRANCH * D_MODEL), lambda i: (i, 0)),
            pl.BlockSpec((tm, D_MODEL), lambda i: (i, 0)),
            _resident((half, D_MODEL)),
            _resident((half, D_MODEL)),
            _resident((half, D_MODEL)),
            _resident((D_MODEL, D_MODEL)),
        ],
        out_specs=pl.BlockSpec((tm, D_MODEL), lambda i: (i, 0)),
        out_shape=jax.ShapeDtypeStruct((t, D_MODEL), F32),
        compiler_params=_params(("parallel",)),
        name="merge_out_proj",
    )(attn, conv, sgu, gates, x2, wa, wc, ws, wo)


def _ffn_kernel(x_ref, g_ref, wg_ref, wu_ref, wd_ref, gf_ref, o_ref, h_ref, *, final_norm):
    j = pl.program_id(1)

    @pl.when(j == 0)
    def _():
        x = x_ref[...]
        h_ref[...] = _rms(x, g_ref[...]).astype(BF16)
        o_ref[...] = x

    h = h_ref[...]
    a = _dot(h, wg_ref[...])
    b = _dot(h, wu_ref[...])
    mid = (_silu(a) * b).astype(BF16)
    o_ref[...] += _dot(mid, wd_ref[...])

    if final_norm:
        @pl.when(j == pl.num_programs(1) - 1)
        def _():
            o_ref[...] = _rms(o_ref[...], gf_ref[...])


def _ffn_call(x2, g, wg, wu, wd, g_final, final_norm, tm=1024, tf=512):
    t = x2.shape[0]
    d_ff = wg.shape[1]
    return pl.pallas_call(
        functools.partial(_ffn_kernel, final_norm=final_norm),
        grid=(t // tm, d_ff // tf),
        in_specs=[
            pl.BlockSpec((tm, D_MODEL), lambda i, j: (i, 0)),
            _resident((1, D_MODEL)),
            pl.BlockSpec((D_MODEL, tf), lambda i, j: (0, j)),
            pl.BlockSpec((D_MODEL, tf), lambda i, j: (0, j)),
            pl.BlockSpec((tf, D_MODEL), lambda i, j: (j, 0)),
            _resident((1, D_MODEL)),
        ],
        out_specs=pl.BlockSpec((tm, D_MODEL), lambda i, j: (i, 0)),
        out_shape=jax.ShapeDtypeStruct((t, D_MODEL), F32),
        scratch_shapes=[pltpu.VMEM((tm, D_MODEL), BF16)],
        compiler_params=_params(("parallel", "arbitrary")),
        name="swiglu_ffn",
    )(x2, g, wg, wu, wd, g_final)


def _rope_tables(seq):
    pos = jnp.arange(seq, dtype=jnp.int32)
    row = (pos // GRID_W).astype(F32)
    col = (pos % GRID_W).astype(F32)
    inv = ROPE_THETA ** (-jnp.arange(ROPE_FREQ_PER_AXIS, dtype=F32) / ROPE_FREQ_PER_AXIS)
    ang = jnp.concatenate([row[:, None] * inv, col[:, None] * inv], axis=-1)
    cos, sin = jnp.cos(ang), jnp.sin(ang)
    return jnp.concatenate([cos, cos], axis=-1), jnp.concatenate([-sin, sin], axis=-1)


def _gained_tables(cos2, sin2, gain, scale):
    return cos2 * (gain * scale), sin2 * (jnp.roll(gain, ROPE_PAIRS) * scale)


def kernel(x, g_mix, w_in, b_gate, q_norm_g, k_norm_g, w_attn_o, w_dw, b_dw, conv_ln_g, conv_ln_b, w_conv_o, sg_ln_g, sg_ln_b, w_s, b_s, w_sg_o, w_out, g_ffn, w_ff_gate, w_ff_up, w_ff_down, g_final):
    batch, seq, _ = x.shape
    depth = w_in.shape[0]
    cos2, sin2 = _rope_tables(seq)
    x2 = x.reshape(batch * seq, D_MODEL)
    row = lambda p: p.reshape(1, -1)
    w_in_bf16 = _cast_call(w_in, 0)
    for l in range(depth):
        h, qkv = _qkv_call(x2, row(g_mix[l]), w_in_bf16)
        z = _glu_call(h, w_in_bf16)
        sgu = _sgu_call(h, w_in_bf16, row(sg_ln_g[l]), row(sg_ln_b[l]), w_s[l], b_s[l].T)
        gates = _gates_call(h, w_in_bf16, row(b_gate[l]))
        tables = (*_gained_tables(cos2, sin2, q_norm_g[l], HEAD_DIM ** -0.5 * LOG2_E),
                  *_gained_tables(cos2, sin2, k_norm_g[l], 1.0))
        ride = [(w, l) for w in (w_attn_o, w_conv_o, w_sg_o, w_out, w_ff_gate, w_ff_up, w_ff_down)]
        if l + 1 < depth:
            ride.append((w_in, l + 1))
        attn, staged = _attn_call(qkv, tables, ride, batch, seq)
        w_a, w_c, w_s_o, w_o, w_fg, w_fu, w_fd = staged[:7]
        conv = _conv_call(z, w_dw[l].reshape(CONV_WIDTH, CONV_CH), row(b_dw[l]),
                          row(conv_ln_g[l]), row(conv_ln_b[l]), batch, seq)
        x2 = _merge_call(attn, conv, sgu, gates, x2, w_a, w_c, w_s_o, w_o)
        x2 = _ffn_call(x2, row(g_ffn[l]), w_fg, w_fu, w_fd, row(g_final), final_norm=(l == depth - 1))
        if l + 1 < depth:
            w_in_bf16 = staged[7]
    return x2.reshape(batch, seq, D_MODEL)
```

```python
import functools

import jax
import jax.numpy as jnp
from jax import lax
from jax.experimental import pallas as pl
from jax.experimental.pallas import tpu as pltpu

D_MODEL = 2048
GRID_W = 64
HEAD_DIM = 128
N_Q_HEADS = 8
N_KV_HEADS = 2
Q_PER_KV = N_Q_HEADS // N_KV_HEADS
ROPE_THETA = 10000.0
ROPE_PAIRS = HEAD_DIM // 2
ROPE_FREQ_PER_AXIS = ROPE_PAIRS // 2
CONV_CH = 1024
CONV_WIDTH = 31
CONV_PAD = CONV_WIDTH // 2
SG_CH = 1024
SG_GROUP_CH = 128
SG_GROUPS = SG_CH // SG_GROUP_CH
SG_CHUNK = 128
N_BRANCH = 3
Q_COLS = N_Q_HEADS * HEAD_DIM
KV_COLS = N_KV_HEADS * HEAD_DIM
QKV_COLS = Q_COLS + 2 * KV_COLS
RMS_EPS = 1e-6
LN_EPS = 1e-5
LOG2_E = 1.4426950408889634
IN_COLS = QKV_COLS + 2 * CONV_CH + 2 * SG_CH + N_BRANCH * D_MODEL
W_IN_COL_BLOCK = 512

V7X_VMEM_BYTES = 64 * 1024 * 1024
V7X_SUBLANES = 8
V7X_LANES = 128
BF16_SUBLANES = 16
CONV_HALO = -(-CONV_PAD // V7X_SUBLANES) * V7X_SUBLANES

BF16 = jnp.bfloat16
F32 = jnp.float32


def _params(semantics):
    return pltpu.CompilerParams(dimension_semantics=semantics, vmem_limit_bytes=V7X_VMEM_BYTES)


def _resident(shape):
    return pl.BlockSpec(shape, lambda *_: (0,) * len(shape), pipeline_mode=pl.Buffered(1))


def _w_in_cols(first_block, n_blocks):
    return [pl.BlockSpec((D_MODEL, W_IN_COL_BLOCK), functools.partial(lambda j, *_: (0, j), first_block + k),
                         pipeline_mode=pl.Buffered(1)) for k in range(n_blocks)]


def _rms(xf, gain):
    return xf * lax.rsqrt(jnp.mean(xf * xf, axis=-1, keepdims=True) + RMS_EPS) * gain


def _layer_norm(xf, gain, bias):
    mu = jnp.mean(xf, axis=-1, keepdims=True)
    xc = xf - mu
    return xc * lax.rsqrt(jnp.mean(xc * xc, axis=-1, keepdims=True) + LN_EPS) * gain + bias


def _sigmoid(x):
    return 0.5 * jnp.tanh(0.5 * x) + 0.5


def _silu(x):
    h = 0.5 * x
    return h + h * jnp.tanh(h)


def _dot(a, b):
    return jnp.dot(a, b, preferred_element_type=F32)


CAST_COL_BLOCK = 512
CAST_MAX_ROW_BLOCK = 2048


def _cast_kernel(w_ref, o_ref):
    o_ref[...] = w_ref[...].astype(BF16)


def _cast_call(w, layer):
    _, rows, cols = w.shape
    assert cols % CAST_COL_BLOCK == 0
    row_blk = next(r for r in range(min(rows, CAST_MAX_ROW_BLOCK), 0, -V7X_SUBLANES) if rows % r == 0)
    return pl.pallas_call(
        _cast_kernel,
        grid=(rows // row_blk, cols // CAST_COL_BLOCK),
        in_specs=[pl.BlockSpec((None, row_blk, CAST_COL_BLOCK), lambda i, j: (layer, i, j))],
        out_specs=pl.BlockSpec((row_blk, CAST_COL_BLOCK), lambda i, j: (i, j)),
        out_shape=jax.ShapeDtypeStruct((rows, cols), BF16),
        compiler_params=_params(("parallel", "parallel")),
        name="weight_to_bf16",
    )(w)


def _qkv_kernel(x_ref, g_ref, w_ref, h_ref, qkv_ref):
    h = _rms(x_ref[...], g_ref[...]).astype(BF16)
    h_ref[...] = h
    qkv_ref[...] = _dot(h, w_ref[...]).astype(BF16)


def _qkv_call(x2, g, w, tm=1024):
    t = x2.shape[0]
    return pl.pallas_call(
        _qkv_kernel,
        grid=(t // tm,),
        in_specs=[
            pl.BlockSpec((tm, D_MODEL), lambda i: (i, 0)),
            _resident((1, D_MODEL)),
            pl.BlockSpec((D_MODEL, QKV_COLS), lambda i: (0, 0), pipeline_mode=pl.Buffered(1)),
        ],
        out_specs=[
            pl.BlockSpec((tm, D_MODEL), lambda i: (i, 0)),
            pl.BlockSpec((tm, QKV_COLS), lambda i: (i, 0)),
        ],
        out_shape=[
            jax.ShapeDtypeStruct((t, D_MODEL), BF16),
            jax.ShapeDtypeStruct((t, QKV_COLS), BF16),
        ],
        compiler_params=_params(("parallel",)),
        name="rms_qkv",
    )(x2, g, w)


def _glu_kernel(h_ref, *refs):
    *w_refs, z_ref = refs
    n = len(w_refs) // 2
    h = h_ref[...]
    for k in range(n):
        cols = slice(k * W_IN_COL_BLOCK, (k + 1) * W_IN_COL_BLOCK)
        z_ref[:, cols] = (_dot(h, w_refs[k][...]) * _sigmoid(_dot(h, w_refs[n + k][...]))).astype(BF16)


def _glu_call(h, w_in_bf16, tm=1024):
    t = h.shape[0]
    first = QKV_COLS // W_IN_COL_BLOCK
    n_blocks = 2 * CONV_CH // W_IN_COL_BLOCK
    return pl.pallas_call(
        _glu_kernel,
        grid=(t // tm,),
        in_specs=[pl.BlockSpec((tm, D_MODEL), lambda i: (i, 0)), *_w_in_cols(first, n_blocks)],
        out_specs=pl.BlockSpec((tm, CONV_CH), lambda i: (i, 0)),
        out_shape=jax.ShapeDtypeStruct((t, CONV_CH), BF16),
        compiler_params=_params(("parallel",)),
        name="conv_glu_in",
    )(h, *([w_in_bf16] * n_blocks))


def _sgu_kernel(h_ref, *refs):
    *w_refs, g_ref, b_ref, ws_ref, bst_ref, o_ref = refs
    n = len(w_refs) // 2
    h = h_ref[...]
    project = lambda blocks: jax.nn.gelu(jnp.concatenate([_dot(h, w[...]) for w in blocks], axis=1))
    u = project(w_refs[:n])
    v = project(w_refs[n:])
    for grp in range(SG_GROUPS):
        cols = slice(grp * SG_GROUP_CH, (grp + 1) * SG_GROUP_CH)
        vn = _layer_norm(v[:, cols], g_ref[:, cols], b_ref[:, cols]).astype(BF16)
        w = ws_ref[grp].astype(BF16)
        bias = bst_ref[:, grp:grp + 1]
        for r0 in range(0, h.shape[0], SG_CHUNK):
            rows = slice(r0, r0 + SG_CHUNK)
            o_ref[rows, cols] = (u[rows, cols] * (_dot(w, vn[rows, :]) + bias)).astype(BF16)


def _sgu_call(h, w_in_bf16, ln_g, ln_b, ws, bst, tm=1024):
    t = h.shape[0]
    assert tm % SG_CHUNK == 0
    first = (QKV_COLS + 2 * CONV_CH) // W_IN_COL_BLOCK
    n_blocks = 2 * SG_CH // W_IN_COL_BLOCK
    return pl.pallas_call(
        _sgu_kernel,
        grid=(t // tm,),
        in_specs=[
            pl.BlockSpec((tm, D_MODEL), lambda i: (i, 0)),
            *_w_in_cols(first, n_blocks),
            _resident((1, SG_CH)),
            _resident((1, SG_CH)),
            _resident((SG_GROUPS, SG_CHUNK, SG_CHUNK)),
            _resident((SG_CHUNK, SG_GROUPS)),
        ],
        out_specs=pl.BlockSpec((tm, SG_CH), lambda i: (i, 0)),
        out_shape=jax.ShapeDtypeStruct((t, SG_CH), BF16),
        compiler_params=_params(("parallel",)),
        name="spatial_gating",
    )(h, *([w_in_bf16] * n_blocks), ln_g, ln_b, ws, bst)


def _gates_kernel(h_ref, *refs):
    *w_refs, b_ref, o_ref = refs
    h = h_ref[...]
    for k, w_ref in enumerate(w_refs):
        cols = slice(k * W_IN_COL_BLOCK, (k + 1) * W_IN_COL_BLOCK)
        o_ref[:, cols] = _sigmoid(_dot(h, w_ref[...]) + b_ref[:, cols]).astype(BF16)


def _gates_call(h, w_in_bf16, b, tm=1024):
    t = h.shape[0]
    n = N_BRANCH * D_MODEL
    first = (IN_COLS - n) // W_IN_COL_BLOCK
    n_blocks = n // W_IN_COL_BLOCK
    return pl.pallas_call(
        _gates_kernel,
        grid=(t // tm,),
        in_specs=[pl.BlockSpec((tm, D_MODEL), lambda i: (i, 0)), *_w_in_cols(first, n_blocks),
                  _resident((1, n))],
        out_specs=pl.BlockSpec((tm, n), lambda i: (i, 0)),
        out_shape=jax.ShapeDtypeStruct((t, n), BF16),
        compiler_params=_params(("parallel",)),
        name="branch_gates",
    )(h, *([w_in_bf16] * n_blocks), b)


def _norm_rope(y, cos, sin):
    y = y.astype(F32)
    rinv = lax.rsqrt(jnp.mean(y * y, axis=-1, keepdims=True) + RMS_EPS)
    return ((y * cos + pltpu.roll(y, ROPE_PAIRS, axis=1) * sin) * rinv).astype(BF16)


def _attn_kernel(q_ref, k_ref, v_ref, qcos_ref, qsin_ref, kcos_ref, ksin_ref, *rest, sub_rows, n_ride):
    ride_in, (o_ref, *ride_out), k_scr = rest[:n_ride], rest[n_ride:-1], rest[-1]
    for src, dst in zip(ride_in, ride_out):
        if len(dst.shape) == 2:
            dst[...] = src[...].astype(BF16)
        else:
            width = dst.shape[2]
            for j in range(dst.shape[0]):
                dst[j] = src[:, j * width:(j + 1) * width].astype(BF16)

    @pl.when(pl.program_id(2) == 0)
    def _():
        k_scr[...] = _norm_rope(k_ref[...], kcos_ref[...], ksin_ref[...])

    k = k_scr[...]
    v = v_ref[...]
    v_ext = jnp.concatenate([v, jnp.ones_like(v)], axis=1)
    tq = q_ref.shape[0]
    for r0 in range(0, tq, sub_rows):
        rows = slice(r0, r0 + sub_rows)
        for grp in range(Q_PER_KV):
            lo = grp * HEAD_DIM
            q = _norm_rope(q_ref[rows, lo:lo + HEAD_DIM], qcos_ref[rows, :], qsin_ref[rows, :])
            s = lax.dot_general(q, k, (((1,), (1,)), ((), ())), preferred_element_type=F32)
            p = jnp.exp2(s - jnp.max(s, axis=-1, keepdims=True)).astype(BF16)
            o = _dot(p, v_ext)
            o_ref[rows, lo:lo + HEAD_DIM] = (o[:, :HEAD_DIM] / o[:, HEAD_DIM:]).astype(BF16)


def _attn_call(qkv, tables, ride, batch, seq, tq=1024):
    t = qkv.shape[0]
    q_blocks = seq // tq
    grp_cols = Q_PER_KV * HEAD_DIM
    k_col0 = Q_COLS // HEAD_DIM
    v_col0 = (Q_COLS + KV_COLS) // HEAD_DIM
    q_table = pl.BlockSpec((tq, HEAD_DIM), lambda b, h, i: (i, 0))
    k_table = pl.BlockSpec((seq, HEAD_DIM), lambda b, h, i: (0, 0))
    n_steps = batch * N_KV_HEADS * q_blocks
    step = lambda b, h, i: (b * N_KV_HEADS + h) * q_blocks + i
    ride_in, ride_out, ride_shapes = [], [], []
    for w, layer, col_block in ride:
        _, rows, cols = w.shape
        slab = rows // n_steps
        assert slab * n_steps == rows and slab % BF16_SUBLANES == 0
        ride_in.append(pl.BlockSpec((None, slab, cols), functools.partial(
            lambda layer, b, h, i: (layer, step(b, h, i), 0), layer)))
        if col_block is None:
            ride_out.append(pl.BlockSpec((slab, cols), lambda b, h, i: (step(b, h, i), 0)))
            ride_shapes.append(jax.ShapeDtypeStruct((rows, cols), BF16))
        else:
            ride_out.append(pl.BlockSpec((cols // col_block, slab, col_block), lambda b, h, i: (0, step(b, h, i), 0)))
            ride_shapes.append(jax.ShapeDtypeStruct((cols // col_block, rows, col_block), BF16))
    out = pl.pallas_call(
        functools.partial(_attn_kernel, sub_rows=256, n_ride=len(ride)),
        grid=(batch, N_KV_HEADS, q_blocks),
        in_specs=[
            pl.BlockSpec((tq, grp_cols), lambda b, h, i: (b * q_blocks + i, h)),
            pl.BlockSpec((seq, HEAD_DIM), lambda b, h, i: (b, k_col0 + h)),
            pl.BlockSpec((seq, HEAD_DIM), lambda b, h, i: (b, v_col0 + h)),
            q_table, q_table, k_table, k_table, *ride_in,
        ],
        out_specs=[pl.BlockSpec((tq, grp_cols), lambda b, h, i: (b * q_blocks + i, h)), *ride_out],
        out_shape=[jax.ShapeDtypeStruct((t, Q_COLS), BF16), *ride_shapes],
        scratch_shapes=[pltpu.VMEM((seq, HEAD_DIM), BF16)],
        compiler_params=_params(("arbitrary", "arbitrary", "arbitrary")),
        name="gqa_attention",
    )(qkv, qkv, qkv, *tables, *(w for w, _, _ in ride))
    return out[0], out[1:]


def _conv_kernel(z_ref, w_ref, b_ref, g_ref, beta_ref, o_ref, pad_ref, conv_ref, *, seq, rows,
                 tiles_per_step):
    zeros = jnp.zeros((CONV_HALO, CONV_CH), BF16)
    pad_ref[0:CONV_HALO, :] = zeros
    pad_ref[CONV_HALO + seq:, :] = zeros
    pad_ref[CONV_HALO:CONV_HALO + seq, :] = z_ref[...]
    win_rows = rows + 2 * CONV_HALO
    lane_blocks = CONV_CH // V7X_LANES
    row_groups = 2 * CONV_HALO // V7X_SUBLANES
    n_shifts = V7X_SUBLANES - 1
    r = lax.broadcasted_iota(jnp.int32, (n_shifts * win_rows, win_rows), 0)
    c = lax.broadcasted_iota(jnp.int32, (n_shifts * win_rows, win_rows), 1)
    shift_mat = (c == r % win_rows + r // win_rows + 1).astype(BF16)

    half_g = 0.5 * g_ref[...]
    half_beta = 0.5 * beta_ref[...]

    def conv_tile(r0, tile_ref):
        win = pad_ref[pl.ds(r0, win_rows), :]
        shifted_all = _dot(shift_mat, win)
        win_f32 = win.astype(F32)
        for cb in range(lane_blocks):
            lanes = slice(cb * V7X_LANES, (cb + 1) * V7X_LANES)
            acc = jnp.zeros((rows, V7X_LANES), F32)
            for sub in range(V7X_SUBLANES):
                base = (sub - 1) * win_rows
                for grp in range(row_groups):
                    tap = V7X_SUBLANES * grp + sub - (CONV_HALO - CONV_PAD)
                    if 0 <= tap < CONV_WIDTH:
                        lo = V7X_SUBLANES * grp
                        src = (win_f32[lo:lo + rows, lanes] if sub == 0
                               else shifted_all[base + lo:base + lo + rows, lanes])
                        acc = acc + src * w_ref[tap:tap + 1, lanes]
            tile_ref[:, lanes] = acc + b_ref[:, lanes]
        h = _layer_norm(tile_ref[...], half_g, half_beta)
        o_ref[pl.ds(r0, rows), :] = (h + h * jnp.tanh(h)).astype(BF16)

    def step(i, carry):
        for u in range(tiles_per_step):
            conv_tile(pl.multiple_of((i * tiles_per_step + u) * rows, rows), conv_ref.at[u])
        return carry

    lax.fori_loop(0, seq // (rows * tiles_per_step), step, 0)


def _conv_call(z, w_dw, b_dw, ln_g, ln_b, batch, seq, rows=64, tiles_per_step=4):
    t = z.shape[0]
    return pl.pallas_call(
        functools.partial(_conv_kernel, seq=seq, rows=rows, tiles_per_step=tiles_per_step),
        grid=(batch,),
        in_specs=[
            pl.BlockSpec((seq, CONV_CH), lambda b: (b, 0)),
            _resident((CONV_WIDTH, CONV_CH)),
            _resident((1, CONV_CH)),
            _resident((1, CONV_CH)),
            _resident((1, CONV_CH)),
        ],
        out_specs=pl.BlockSpec((seq, CONV_CH), lambda b: (b, 0)),
        out_shape=jax.ShapeDtypeStruct((t, CONV_CH), BF16),
        scratch_shapes=[pltpu.VMEM((seq + 2 * CONV_HALO, CONV_CH), BF16),
                        pltpu.VMEM((tiles_per_step, rows, CONV_CH), F32)],
        compiler_params=_params(("parallel",)),
        name="conv_ln_swish",
    )(z, w_dw, b_dw, ln_g, ln_b)


def _merge_kernel(a_ref, c_ref, s_ref, gate_ref, x_ref, wa_ref, wc_ref, ws_ref, wo_ref, o_ref):
    merged = gate_ref[:, 0:D_MODEL].astype(F32) * _dot(a_ref[...], wa_ref[...])
    merged = merged + gate_ref[:, D_MODEL:2 * D_MODEL].astype(F32) * _dot(c_ref[...], wc_ref[...])
    merged = merged + gate_ref[:, 2 * D_MODEL:].astype(F32) * _dot(s_ref[...], ws_ref[...])
    o_ref[...] = x_ref[...] + _dot(merged.astype(BF16), wo_ref[...])


def _merge_call(attn, conv, sgu, gates, x2, wa, wc, ws, wo, tm=512):
    t = x2.shape[0]
    half = D_MODEL // 2
    return pl.pallas_call(
        _merge_kernel,
        grid=(t // tm,),
        in_specs=[
            pl.BlockSpec((tm, half), lambda i: (i, 0)),
            pl.BlockSpec((tm, half), lambda i: (i, 0)),
            pl.BlockSpec((tm, half), lambda i: (i, 0)),
            pl.BlockSpec((tm, N_BRANCH * D_MODEL), lambda i: (i, 0)),
            pl.BlockSpec((tm, D_MODEL), lambda i: (i, 0)),
            _resident((half, D_MODEL)),
            _resident((half, D_MODEL)),
            _resident((half, D_MODEL)),
            _resident((D_MODEL, D_MODEL)),
        ],
        out_specs=pl.BlockSpec((tm, D_MODEL), lambda i: (i, 0)),
        out_shape=jax.ShapeDtypeStruct((t, D_MODEL), F32),
        compiler_params=_params(("parallel",)),
        name="merge_out_proj",
    )(attn, conv, sgu, gates, x2, wa, wc, ws, wo)


FFN_COL_BLOCK = 512


def _ffn_kernel(x_ref, g_ref, wg_ref, wu_ref, wd_ref, gf_ref, o_ref, h_ref, *, final_norm):
    j = pl.program_id(1)

    @pl.when(j == 0)
    def _():
        x = x_ref[...]
        h_ref[...] = _rms(x, g_ref[...]).astype(BF16)
        o_ref[...] = x

    h = h_ref[...]
    a = _dot(h, wg_ref[...])
    b = _dot(h, wu_ref[...])
    mid = (_silu(a) * b).astype(BF16)
    o_ref[...] += _dot(mid, wd_ref[...])

    if final_norm:
        @pl.when(j == pl.num_programs(1) - 1)
        def _():
            o_ref[...] = _rms(o_ref[...], gf_ref[...])


def _ffn_call(x2, g, wg, wu, wd, g_final, final_norm, tm=1024):
    t = x2.shape[0]
    tf = wg.shape[2]
    d_ff = wd.shape[0]
    return pl.pallas_call(
        functools.partial(_ffn_kernel, final_norm=final_norm),
        grid=(t // tm, d_ff // tf),
        in_specs=[
            pl.BlockSpec((tm, D_MODEL), lambda i, j: (i, 0)),
            _resident((1, D_MODEL)),
            pl.BlockSpec((None, D_MODEL, tf), lambda i, j: (j, 0, 0)),
            pl.BlockSpec((None, D_MODEL, tf), lambda i, j: (j, 0, 0)),
            pl.BlockSpec((tf, D_MODEL), lambda i, j: (j, 0)),
            _resident((1, D_MODEL)),
        ],
        out_specs=pl.BlockSpec((tm, D_MODEL), lambda i, j: (i, 0)),
        out_shape=jax.ShapeDtypeStruct((t, D_MODEL), F32),
        scratch_shapes=[pltpu.VMEM((tm, D_MODEL), BF16)],
        compiler_params=_params(("parallel", "arbitrary")),
        name="swiglu_ffn",
    )(x2, g, wg, wu, wd, g_final)


def _rope_tables(seq):
    pos = jnp.arange(seq, dtype=jnp.int32)
    row = (pos // GRID_W).astype(F32)
    col = (pos % GRID_W).astype(F32)
    inv = ROPE_THETA ** (-jnp.arange(ROPE_FREQ_PER_AXIS, dtype=F32) / ROPE_FREQ_PER_AXIS)
    ang = jnp.concatenate([row[:, None] * inv, col[:, None] * inv], axis=-1)
    cos, sin = jnp.cos(ang), jnp.sin(ang)
    return jnp.concatenate([cos, cos], axis=-1), jnp.concatenate([-sin, sin], axis=-1)


def _gained_tables(cos2, sin2, gain, scale):
    return cos2 * (gain * scale), sin2 * (jnp.roll(gain, ROPE_PAIRS) * scale)


def kernel(x, g_mix, w_in, b_gate, q_norm_g, k_norm_g, w_attn_o, w_dw, b_dw, conv_ln_g, conv_ln_b, w_conv_o, sg_ln_g, sg_ln_b, w_s, b_s, w_sg_o, w_out, g_ffn, w_ff_gate, w_ff_up, w_ff_down, g_final):
    batch, seq, _ = x.shape
    depth = w_in.shape[0]
    cos2, sin2 = _rope_tables(seq)
    x2 = x.reshape(batch * seq, D_MODEL)
    row = lambda p: p.reshape(1, -1)
    w_in_bf16 = _cast_call(w_in, 0)
    for l in range(depth):
        h, qkv = _qkv_call(x2, row(g_mix[l]), w_in_bf16)
        z = _glu_call(h, w_in_bf16)
        sgu = _sgu_call(h, w_in_bf16, row(sg_ln_g[l]), row(sg_ln_b[l]), w_s[l], b_s[l].T)
        gates = _gates_call(h, w_in_bf16, row(b_gate[l]))
        tables = (*_gained_tables(cos2, sin2, q_norm_g[l], HEAD_DIM ** -0.5 * LOG2_E),
                  *_gained_tables(cos2, sin2, k_norm_g[l], 1.0))
        ride = [(w, l, None) for w in (w_attn_o, w_conv_o, w_sg_o, w_out)]
        ride += [(w_ff_gate, l, FFN_COL_BLOCK), (w_ff_up, l, FFN_COL_BLOCK), (w_ff_down, l, None)]
        if l + 1 < depth:
            ride.append((w_in, l + 1, None))
        attn, staged = _attn_call(qkv, tables, ride, batch, seq)
        w_a, w_c, w_s_o, w_o, w_fg, w_fu, w_fd = staged[:7]
        conv = _conv_call(z, w_dw[l].reshape(CONV_WIDTH, CONV_CH), row(b_dw[l]),
                          row(conv_ln_g[l]), row(conv_ln_b[l]), batch, seq)
        x2 = _merge_call(attn, conv, sgu, gates, x2, w_a, w_c, w_s_o, w_o)
        x2 = _ffn_call(x2, row(g_ffn[l]), w_fg, w_fu, w_fd, row(g_final), final_norm=(l == depth - 1))
        if l + 1 < depth:
            w_in_bf16 = staged[7]
    return x2.reshape(batch, seq, D_MODEL)
```

```python
import functools

import jax
import jax.numpy as jnp
from jax import lax
from jax.experimental import pallas as pl
from jax.experimental.pallas import tpu as pltpu

D_MODEL = 2048
GRID_W = 64
HEAD_DIM = 128
N_Q_HEADS = 8
N_KV_HEADS = 2
Q_PER_KV = N_Q_HEADS // N_KV_HEADS
ROPE_THETA = 10000.0
ROPE_PAIRS = HEAD_DIM // 2
ROPE_FREQ_PER_AXIS = ROPE_PAIRS // 2
CONV_CH = 1024
CONV_WIDTH = 31
CONV_PAD = CONV_WIDTH // 2
SG_CH = 1024
SG_GROUP_CH = 128
SG_GROUPS = SG_CH // SG_GROUP_CH
SG_CHUNK = 128
N_BRANCH = 3
Q_COLS = N_Q_HEADS * HEAD_DIM
KV_COLS = N_KV_HEADS * HEAD_DIM
QKV_COLS = Q_COLS + 2 * KV_COLS
RMS_EPS = 1e-6
LN_EPS = 1e-5
LOG2_E = 1.4426950408889634
IN_COLS = QKV_COLS + 2 * CONV_CH + 2 * SG_CH + N_BRANCH * D_MODEL
W_IN_COL_BLOCK = 512

V7X_VMEM_BYTES = 64 * 1024 * 1024
V7X_SUBLANES = 8
V7X_LANES = 128
BF16_SUBLANES = 16
CONV_HALO = -(-CONV_PAD // V7X_SUBLANES) * V7X_SUBLANES

BF16 = jnp.bfloat16
F32 = jnp.float32


def _params(semantics):
    return pltpu.CompilerParams(dimension_semantics=semantics, vmem_limit_bytes=V7X_VMEM_BYTES)


def _resident(shape):
    return pl.BlockSpec(shape, lambda *_: (0,) * len(shape), pipeline_mode=pl.Buffered(1))


def _w_in_cols(first_block, n_blocks):
    return [pl.BlockSpec((D_MODEL, W_IN_COL_BLOCK), functools.partial(lambda j, *_: (0, j), first_block + k),
                         pipeline_mode=pl.Buffered(1)) for k in range(n_blocks)]


def _rms(xf, gain):
    return xf * lax.rsqrt(jnp.mean(xf * xf, axis=-1, keepdims=True) + RMS_EPS) * gain


def _layer_norm(xf, gain, bias):
    mu = jnp.mean(xf, axis=-1, keepdims=True)
    xc = xf - mu
    return xc * lax.rsqrt(jnp.mean(xc * xc, axis=-1, keepdims=True) + LN_EPS) * gain + bias


def _sigmoid(x):
    return 0.5 * jnp.tanh(0.5 * x) + 0.5


def _silu(x):
    h = 0.5 * x
    return h + h * jnp.tanh(h)


def _dot(a, b):
    return jnp.dot(a, b, preferred_element_type=F32)


CAST_COL_BLOCK = 512
CAST_MAX_ROW_BLOCK = 2048


def _cast_kernel(w_ref, o_ref):
    o_ref[...] = w_ref[...].astype(BF16)


def _cast_call(w, layer):
    _, rows, cols = w.shape
    assert cols % CAST_COL_BLOCK == 0
    row_blk = next(r for r in range(min(rows, CAST_MAX_ROW_BLOCK), 0, -V7X_SUBLANES) if rows % r == 0)
    return pl.pallas_call(
        _cast_kernel,
        grid=(rows // row_blk, cols // CAST_COL_BLOCK),
        in_specs=[pl.BlockSpec((None, row_blk, CAST_COL_BLOCK), lambda i, j: (layer, i, j))],
        out_specs=pl.BlockSpec((row_blk, CAST_COL_BLOCK), lambda i, j: (i, j)),
        out_shape=jax.ShapeDtypeStruct((rows, cols), BF16),
        compiler_params=_params(("parallel", "parallel")),
        name="weight_to_bf16",
    )(w)


def _qkv_kernel(x_ref, g_ref, w_ref, h_ref, qkv_ref):
    h = _rms(x_ref[...], g_ref[...]).astype(BF16)
    h_ref[...] = h
    qkv_ref[...] = _dot(h, w_ref[...]).astype(BF16)


def _qkv_call(x2, g, w, tm=1024):
    t = x2.shape[0]
    return pl.pallas_call(
        _qkv_kernel,
        grid=(t // tm,),
        in_specs=[
            pl.BlockSpec((tm, D_MODEL), lambda i: (i, 0)),
            _resident((1, D_MODEL)),
            pl.BlockSpec((D_MODEL, QKV_COLS), lambda i: (0, 0), pipeline_mode=pl.Buffered(1)),
        ],
        out_specs=[
            pl.BlockSpec((tm, D_MODEL), lambda i: (i, 0)),
            pl.BlockSpec((tm, QKV_COLS), lambda i: (i, 0)),
        ],
        out_shape=[
            jax.ShapeDtypeStruct((t, D_MODEL), BF16),
            jax.ShapeDtypeStruct((t, QKV_COLS), BF16),
        ],
        compiler_params=_params(("parallel",)),
        name="rms_qkv",
    )(x2, g, w)


def _glu_sgu_kernel(h_ref, *refs):
    *w_refs, g_ref, b_ref, ws_ref, bst_ref, z_ref, o_ref = refs
    n = len(w_refs) // 4
    wa, wg, wu, wv = (w_refs[k * n:(k + 1) * n] for k in range(4))
    h = h_ref[...]
    for k in range(n):
        cols = slice(k * W_IN_COL_BLOCK, (k + 1) * W_IN_COL_BLOCK)
        z_ref[:, cols] = (_dot(h, wa[k][...]) * _sigmoid(_dot(h, wg[k][...]))).astype(BF16)
    project = lambda blocks: jax.nn.gelu(jnp.concatenate([_dot(h, w[...]) for w in blocks], axis=1))
    u = project(wu)
    v = project(wv)
    for grp in range(SG_GROUPS):
        cols = slice(grp * SG_GROUP_CH, (grp + 1) * SG_GROUP_CH)
        vn = _layer_norm(v[:, cols], g_ref[:, cols], b_ref[:, cols]).astype(BF16)
        w = ws_ref[grp].astype(BF16)
        bias = bst_ref[:, grp:grp + 1]
        for r0 in range(0, h.shape[0], SG_CHUNK):
            rows = slice(r0, r0 + SG_CHUNK)
            o_ref[rows, cols] = (u[rows, cols] * (_dot(w, vn[rows, :]) + bias)).astype(BF16)


def _glu_sgu_call(h, w_in_bf16, ln_g, ln_b, ws, bst, tm=1024):
    t = h.shape[0]
    assert tm % SG_CHUNK == 0 and CONV_CH == SG_CH
    first = QKV_COLS // W_IN_COL_BLOCK
    n_blocks = 2 * (CONV_CH + SG_CH) // W_IN_COL_BLOCK
    out_spec = pl.BlockSpec((tm, SG_CH), lambda i: (i, 0))
    return pl.pallas_call(
        _glu_sgu_kernel,
        grid=(t // tm,),
        in_specs=[
            pl.BlockSpec((tm, D_MODEL), lambda i: (i, 0)),
            *_w_in_cols(first, n_blocks),
            _resident((1, SG_CH)),
            _resident((1, SG_CH)),
            _resident((SG_GROUPS, SG_CHUNK, SG_CHUNK)),
            _resident((SG_CHUNK, SG_GROUPS)),
        ],
        out_specs=[out_spec, out_spec],
        out_shape=[jax.ShapeDtypeStruct((t, CONV_CH), BF16), jax.ShapeDtypeStruct((t, SG_CH), BF16)],
        compiler_params=_params(("parallel",)),
        name="glu_and_spatial_gating",
    )(h, *([w_in_bf16] * n_blocks), ln_g, ln_b, ws, bst)


def _gates_kernel(h_ref, *refs):
    *w_refs, b_ref, o_ref = refs
    h = h_ref[...]
    for k, w_ref in enumerate(w_refs):
        cols = slice(k * W_IN_COL_BLOCK, (k + 1) * W_IN_COL_BLOCK)
        o_ref[:, cols] = _sigmoid(_dot(h, w_ref[...]) + b_ref[:, cols]).astype(BF16)


def _gates_call(h, w_in_bf16, b, tm=1024):
    t = h.shape[0]
    n = N_BRANCH * D_MODEL
    first = (IN_COLS - n) // W_IN_COL_BLOCK
    n_blocks = n // W_IN_COL_BLOCK
    return pl.pallas_call(
        _gates_kernel,
        grid=(t // tm,),
        in_specs=[pl.BlockSpec((tm, D_MODEL), lambda i: (i, 0)), *_w_in_cols(first, n_blocks),
                  _resident((1, n))],
        out_specs=pl.BlockSpec((tm, n), lambda i: (i, 0)),
        out_shape=jax.ShapeDtypeStruct((t, n), BF16),
        compiler_params=_params(("parallel",)),
        name="branch_gates",
    )(h, *([w_in_bf16] * n_blocks), b)


def _norm_rope(y, cos, sin):
    y = y.astype(F32)
    rinv = lax.rsqrt(jnp.mean(y * y, axis=-1, keepdims=True) + RMS_EPS)
    return ((y * cos + pltpu.roll(y, ROPE_PAIRS, axis=1) * sin) * rinv).astype(BF16)


def _attn_kernel(q_ref, k_ref, v_ref, qcos_ref, qsin_ref, kcos_ref, ksin_ref, *rest, sub_rows, n_ride):
    ride_in, (o_ref, *ride_out), k_scr = rest[:n_ride], rest[n_ride:-1], rest[-1]
    for src, dst in zip(ride_in, ride_out):
        dst[...] = src[...].astype(BF16)

    @pl.when(pl.program_id(2) == 0)
    def _():
        k_scr[...] = _norm_rope(k_ref[...], kcos_ref[...], ksin_ref[...])

    k = k_scr[...]
    v = v_ref[...]
    v_ext = jnp.concatenate([v, jnp.ones_like(v)], axis=1)
    tq = q_ref.shape[0]
    for r0 in range(0, tq, sub_rows):
        rows = slice(r0, r0 + sub_rows)
        for grp in range(Q_PER_KV):
            lo = grp * HEAD_DIM
            q = _norm_rope(q_ref[rows, lo:lo + HEAD_DIM], qcos_ref[rows, :], qsin_ref[rows, :])
            s = lax.dot_general(q, k, (((1,), (1,)), ((), ())), preferred_element_type=F32)
            p = jnp.exp2(s - jnp.max(s, axis=-1, keepdims=True)).astype(BF16)
            o = _dot(p, v_ext)
            o_ref[rows, lo:lo + HEAD_DIM] = (o[:, :HEAD_DIM] / o[:, HEAD_DIM:]).astype(BF16)


def _attn_call(qkv, tables, ride, batch, seq, tq=1024):
    t = qkv.shape[0]
    q_blocks = seq // tq
    grp_cols = Q_PER_KV * HEAD_DIM
    k_col0 = Q_COLS // HEAD_DIM
    v_col0 = (Q_COLS + KV_COLS) // HEAD_DIM
    q_table = pl.BlockSpec((tq, HEAD_DIM), lambda b, h, i: (i, 0))
    k_table = pl.BlockSpec((seq, HEAD_DIM), lambda b, h, i: (0, 0))
    n_steps = batch * N_KV_HEADS * q_blocks
    step = lambda b, h, i: (b * N_KV_HEADS + h) * q_blocks + i
    ride_in, ride_out, ride_shapes = [], [], []
    for w, layer in ride:
        _, rows, cols = w.shape
        slab = rows // n_steps
        assert slab * n_steps == rows and slab % BF16_SUBLANES == 0
        ride_in.append(pl.BlockSpec((None, slab, cols), functools.partial(
            lambda layer, b, h, i: (layer, step(b, h, i), 0), layer)))
        ride_out.append(pl.BlockSpec((slab, cols), lambda b, h, i: (step(b, h, i), 0)))
        ride_shapes.append(jax.ShapeDtypeStruct((rows, cols), BF16))
    out = pl.pallas_call(
        functools.partial(_attn_kernel, sub_rows=256, n_ride=len(ride)),
        grid=(batch, N_KV_HEADS, q_blocks),
        in_specs=[
            pl.BlockSpec((tq, grp_cols), lambda b, h, i: (b * q_blocks + i, h)),
            pl.BlockSpec((seq, HEAD_DIM), lambda b, h, i: (b, k_col0 + h)),
            pl.BlockSpec((seq, HEAD_DIM), lambda b, h, i: (b, v_col0 + h)),
            q_table, q_table, k_table, k_table, *ride_in,
        ],
        out_specs=[pl.BlockSpec((tq, grp_cols), lambda b, h, i: (b * q_blocks + i, h)), *ride_out],
        out_shape=[jax.ShapeDtypeStruct((t, Q_COLS), BF16), *ride_shapes],
        scratch_shapes=[pltpu.VMEM((seq, HEAD_DIM), BF16)],
        compiler_params=_params(("arbitrary", "arbitrary", "arbitrary")),
        name="gqa_attention",
    )(qkv, qkv, qkv, *tables, *(w for w, _ in ride))
    return out[0], out[1:]


CONV_TILE_ROWS = 64
CONV_WIN_ROWS = CONV_TILE_ROWS + 2 * CONV_HALO


def _shift_matrix():
    n_shifts = V7X_SUBLANES - 1
    r = lax.broadcasted_iota(jnp.int32, (n_shifts * CONV_WIN_ROWS, CONV_WIN_ROWS), 0)
    c = lax.broadcasted_iota(jnp.int32, (n_shifts * CONV_WIN_ROWS, CONV_WIN_ROWS), 1)
    return (c == r % CONV_WIN_ROWS + r // CONV_WIN_ROWS + 1).astype(BF16)


def _conv_tile(win, shift_mat, w_ref, b_ref, half_g, half_beta, tile_ref):
    rows = CONV_TILE_ROWS
    shifted_all = _dot(shift_mat, win)
    win_f32 = win.astype(F32)
    for cb in range(CONV_CH // V7X_LANES):
        lanes = slice(cb * V7X_LANES, (cb + 1) * V7X_LANES)
        acc = jnp.zeros((rows, V7X_LANES), F32)
        for sub in range(V7X_SUBLANES):
            base = (sub - 1) * CONV_WIN_ROWS
            for grp in range(2 * CONV_HALO // V7X_SUBLANES):
                tap = V7X_SUBLANES * grp + sub - (CONV_HALO - CONV_PAD)
                if 0 <= tap < CONV_WIDTH:
                    lo = V7X_SUBLANES * grp
                    src = (win_f32[lo:lo + rows, lanes] if sub == 0
                           else shifted_all[base + lo:base + lo + rows, lanes])
                    acc = acc + src * w_ref[tap:tap + 1, lanes]
        tile_ref[:, lanes] = acc + b_ref[:, lanes]
    h = _layer_norm(tile_ref[...], half_g, half_beta)
    return (h + h * jnp.tanh(h)).astype(BF16)


def _merge_kernel(a_ref, zprev_ref, zcur_ref, znext_ref, s_ref, gate_ref, x_ref, cw_ref, cb_ref, cg_ref,
                  cbeta_ref, wa_ref, wc_ref, ws_ref, wo_ref, o_ref, win_ref, tile_ref, c_ref, *, seq):
    tm = x_ref.shape[0]
    r0 = pl.program_id(0) * tm
    zero = jnp.zeros((CONV_HALO, CONV_CH), BF16)
    win_ref[0:CONV_HALO, :] = jnp.where(r0 % seq == 0, zero, zprev_ref[...])
    win_ref[CONV_HALO:CONV_HALO + tm, :] = zcur_ref[...]
    win_ref[CONV_HALO + tm:, :] = jnp.where((r0 + tm) % seq == 0, zero, znext_ref[...])
    shift_mat = _shift_matrix()
    half_g = 0.5 * cg_ref[...]
    half_beta = 0.5 * cbeta_ref[...]
    for u, base in enumerate(range(0, tm, CONV_TILE_ROWS)):
        c_ref[base:base + CONV_TILE_ROWS, :] = _conv_tile(
            win_ref[base:base + CONV_WIN_ROWS, :], shift_mat, cw_ref, cb_ref, half_g, half_beta, tile_ref.at[u])

    merged = gate_ref[:, 0:D_MODEL].astype(F32) * _dot(a_ref[...], wa_ref[...])
    merged = merged + gate_ref[:, 2 * D_MODEL:].astype(F32) * _dot(s_ref[...], ws_ref[...])
    merged = merged + gate_ref[:, D_MODEL:2 * D_MODEL].astype(F32) * _dot(c_ref[...], wc_ref[...])
    o_ref[...] = x_ref[...] + _dot(merged.astype(BF16), wo_ref[...])


def _merge_call(attn, z, sgu, gates, x2, w_dw, b_dw, ln_g, ln_b, wa, wc, ws, wo, seq, tm=256):
    t = x2.shape[0]
    half = D_MODEL // 2
    assert seq % tm == 0 and tm % CONV_TILE_ROWS == 0
    halo_per_block = tm // CONV_HALO
    last_halo = t // CONV_HALO - 1
    return pl.pallas_call(
        functools.partial(_merge_kernel, seq=seq),
        grid=(t // tm,),
        in_specs=[
            pl.BlockSpec((tm, half), lambda i: (i, 0)),
            pl.BlockSpec((CONV_HALO, CONV_CH), lambda i: (jnp.maximum(i * halo_per_block - 1, 0), 0)),
            pl.BlockSpec((tm, CONV_CH), lambda i: (i, 0)),
            pl.BlockSpec((CONV_HALO, CONV_CH), lambda i: (jnp.minimum((i + 1) * halo_per_block, last_halo), 0)),
            pl.BlockSpec((tm, half), lambda i: (i, 0)),
            pl.BlockSpec((tm, N_BRANCH * D_MODEL), lambda i: (i, 0)),
            pl.BlockSpec((tm, D_MODEL), lambda i: (i, 0)),
            _resident((CONV_WIDTH, CONV_CH)),
            _resident((1, CONV_CH)),
            _resident((1, CONV_CH)),
            _resident((1, CONV_CH)),
            _resident((half, D_MODEL)),
            _resident((half, D_MODEL)),
            _resident((half, D_MODEL)),
            _resident((D_MODEL, D_MODEL)),
        ],
        out_specs=pl.BlockSpec((tm, D_MODEL), lambda i: (i, 0)),
        out_shape=jax.ShapeDtypeStruct((t, D_MODEL), F32),
        scratch_shapes=[pltpu.VMEM((tm + 2 * CONV_HALO, CONV_CH), BF16),
                        pltpu.VMEM((tm // CONV_TILE_ROWS, CONV_TILE_ROWS, CONV_CH), F32),
                        pltpu.VMEM((tm, CONV_CH), BF16)],
        compiler_params=_params(("parallel",)),
        name="conv_merge_out_proj",
    )(attn, z, z, z, sgu, gates, x2, w_dw, b_dw, ln_g, ln_b, wa, wc, ws, wo)


def _ffn_kernel(x_ref, g_ref, wg_ref, wu_ref, wd_ref, gf_ref, o_ref, h_ref, *, final_norm):
    j = pl.program_id(1)

    @pl.when(j == 0)
    def _():
        x = x_ref[...]
        h_ref[...] = _rms(x, g_ref[...]).astype(BF16)
        o_ref[...] = x

    h = h_ref[...]
    a = _dot(h, wg_ref[...])
    b = _dot(h, wu_ref[...])
    mid = (_silu(a) * b).astype(BF16)
    o_ref[...] += _dot(mid, wd_ref[...])

    if final_norm:
        @pl.when(j == pl.num_programs(1) - 1)
        def _():
            o_ref[...] = _rms(o_ref[...], gf_ref[...])


def _ffn_call(x2, g, wg, wu, wd, g_final, final_norm, tm=1024, tf=512):
    t = x2.shape[0]
    d_ff = wg.shape[1]
    return pl.pallas_call(
        functools.partial(_ffn_kernel, final_norm=final_norm),
        grid=(t // tm, d_ff // tf),
        in_specs=[
            pl.BlockSpec((tm, D_MODEL), lambda i, j: (i, 0)),
            _resident((1, D_MODEL)),
            pl.BlockSpec((D_MODEL, tf), lambda i, j: (0, j)),
            pl.BlockSpec((D_MODEL, tf), lambda i, j: (0, j)),
            pl.BlockSpec((tf, D_MODEL), lambda i, j: (j, 0)),
            _resident((1, D_MODEL)),
        ],
        out_specs=pl.BlockSpec((tm, D_MODEL), lambda i, j: (i, 0)),
        out_shape=jax.ShapeDtypeStruct((t, D_MODEL), F32),
        scratch_shapes=[pltpu.VMEM((tm, D_MODEL), BF16)],
        compiler_params=_params(("parallel", "arbitrary")),
        name="swiglu_ffn",
    )(x2, g, wg, wu, wd, g_final)


def _rope_tables(seq):
    pos = jnp.arange(seq, dtype=jnp.int32)
    row = (pos // GRID_W).astype(F32)
    col = (pos % GRID_W).astype(F32)
    inv = ROPE_THETA ** (-jnp.arange(ROPE_FREQ_PER_AXIS, dtype=F32) / ROPE_FREQ_PER_AXIS)
    ang = jnp.concatenate([row[:, None] * inv, col[:, None] * inv], axis=-1)
    cos, sin = jnp.cos(ang), jnp.sin(ang)
    return jnp.concatenate([cos, cos], axis=-1), jnp.concatenate([-sin, sin], axis=-1)


def _gained_tables(cos2, sin2, gain, scale):
    return cos2 * (gain * scale), sin2 * (jnp.roll(gain, ROPE_PAIRS) * scale)


def kernel(x, g_mix, w_in, b_gate, q_norm_g, k_norm_g, w_attn_o, w_dw, b_dw, conv_ln_g, conv_ln_b, w_conv_o, sg_ln_g, sg_ln_b, w_s, b_s, w_sg_o, w_out, g_ffn, w_ff_gate, w_ff_up, w_ff_down, g_final):
    batch, seq, _ = x.shape
    depth = w_in.shape[0]
    cos2, sin2 = _rope_tables(seq)
    x2 = x.reshape(batch * seq, D_MODEL)
    row = lambda p: p.reshape(1, -1)
    w_in_bf16 = _cast_call(w_in, 0)
    for l in range(depth):
        h, qkv = _qkv_call(x2, row(g_mix[l]), w_in_bf16)
        z, sgu = _glu_sgu_call(h, w_in_bf16, row(sg_ln_g[l]), row(sg_ln_b[l]), w_s[l], b_s[l].T)
        gates = _gates_call(h, w_in_bf16, row(b_gate[l]))
        tables = (*_gained_tables(cos2, sin2, q_norm_g[l], HEAD_DIM ** -0.5 * LOG2_E),
                  *_gained_tables(cos2, sin2, k_norm_g[l], 1.0))
        ride = [(w, l) for w in (w_attn_o, w_conv_o, w_sg_o, w_out, w_ff_gate, w_ff_up, w_ff_down)]
        if l + 1 < depth:
            ride.append((w_in, l + 1))
        attn, staged = _attn_call(qkv, tables, ride, batch, seq)
        w_a, w_c, w_s_o, w_o, w_fg, w_fu, w_fd = staged[:7]
        x2 = _merge_call(attn, z, sgu, gates, x2, w_dw[l].reshape(CONV_WIDTH, CONV_CH), row(b_dw[l]),
                         row(conv_ln_g[l]), row(conv_ln_b[l]), w_a, w_c, w_s_o, w_o, seq)
        x2 = _ffn_call(x2, row(g_ffn[l]), w_fg, w_fu, w_fd, row(g_final), final_norm=(l == depth - 1))
        if l + 1 < depth:
            w_in_bf16 = staged[7]
    return x2.reshape(batch, seq, D_MODEL)
```

```python
import functools

import jax
import jax.numpy as jnp
from jax import lax
from jax.experimental import pallas as pl
from jax.experimental.pallas import tpu as pltpu

D_MODEL = 2048
GRID_W = 64
HEAD_DIM = 128
N_Q_HEADS = 8
N_KV_HEADS = 2
Q_PER_KV = N_Q_HEADS // N_KV_HEADS
ROPE_THETA = 10000.0
ROPE_PAIRS = HEAD_DIM // 2
ROPE_FREQ_PER_AXIS = ROPE_PAIRS // 2
CONV_CH = 1024
CONV_WIDTH = 31
CONV_PAD = CONV_WIDTH // 2
SG_CH = 1024
SG_GROUP_CH = 128
SG_GROUPS = SG_CH // SG_GROUP_CH
SG_CHUNK = 128
N_BRANCH = 3
Q_COLS = N_Q_HEADS * HEAD_DIM
KV_COLS = N_KV_HEADS * HEAD_DIM
QKV_COLS = Q_COLS + 2 * KV_COLS
RMS_EPS = 1e-6
LN_EPS = 1e-5
LOG2_E = 1.4426950408889634
IN_COLS = QKV_COLS + 2 * CONV_CH + 2 * SG_CH + N_BRANCH * D_MODEL
W_IN_COL_BLOCK = 512

V7X_VMEM_BYTES = 64 * 1024 * 1024
V7X_SUBLANES = 8
V7X_LANES = 128
BF16_SUBLANES = 16
CONV_HALO = -(-CONV_PAD // V7X_SUBLANES) * V7X_SUBLANES

BF16 = jnp.bfloat16
F32 = jnp.float32


def _params(semantics):
    return pltpu.CompilerParams(dimension_semantics=semantics, vmem_limit_bytes=V7X_VMEM_BYTES)


def _resident(shape):
    return pl.BlockSpec(shape, lambda *_: (0,) * len(shape), pipeline_mode=pl.Buffered(1))


def _w_in_cols(first_block, n_blocks):
    return [pl.BlockSpec((D_MODEL, W_IN_COL_BLOCK), functools.partial(lambda j, *_: (0, j), first_block + k),
                         pipeline_mode=pl.Buffered(1)) for k in range(n_blocks)]


def _rms(xf, gain):
    return xf * lax.rsqrt(jnp.mean(xf * xf, axis=-1, keepdims=True) + RMS_EPS) * gain


def _layer_norm(xf, gain, bias):
    mu = jnp.mean(xf, axis=-1, keepdims=True)
    xc = xf - mu
    return xc * lax.rsqrt(jnp.mean(xc * xc, axis=-1, keepdims=True) + LN_EPS) * gain + bias


def _sigmoid(x):
    return 0.5 * jnp.tanh(0.5 * x) + 0.5


def _silu(x):
    h = 0.5 * x
    return h + h * jnp.tanh(h)


def _dot(a, b):
    return jnp.dot(a, b, preferred_element_type=F32)


CAST_COL_BLOCK = 512
CAST_MAX_ROW_BLOCK = 2048


def _cast_kernel(w_ref, o_ref):
    o_ref[...] = w_ref[...].astype(BF16)


def _cast_call(w, layer):
    _, rows, cols = w.shape
    assert cols % CAST_COL_BLOCK == 0
    row_blk = next(r for r in range(min(rows, CAST_MAX_ROW_BLOCK), 0, -V7X_SUBLANES) if rows % r == 0)
    return pl.pallas_call(
        _cast_kernel,
        grid=(rows // row_blk, cols // CAST_COL_BLOCK),
        in_specs=[pl.BlockSpec((None, row_blk, CAST_COL_BLOCK), lambda i, j: (layer, i, j))],
        out_specs=pl.BlockSpec((row_blk, CAST_COL_BLOCK), lambda i, j: (i, j)),
        out_shape=jax.ShapeDtypeStruct((rows, cols), BF16),
        compiler_params=_params(("parallel", "parallel")),
        name="weight_to_bf16",
    )(w)


def _qkv_kernel(x_ref, g_ref, w_ref, h_ref, qkv_ref):
    h = _rms(x_ref[...], g_ref[...]).astype(BF16)
    h_ref[...] = h
    qkv_ref[...] = _dot(h, w_ref[...]).astype(BF16)


def _qkv_call(x2, g, w, tm=1024):
    t = x2.shape[0]
    return pl.pallas_call(
        _qkv_kernel,
        grid=(t // tm,),
        in_specs=[
            pl.BlockSpec((tm, D_MODEL), lambda i: (i, 0)),
            _resident((1, D_MODEL)),
            pl.BlockSpec((D_MODEL, QKV_COLS), lambda i: (0, 0), pipeline_mode=pl.Buffered(1)),
        ],
        out_specs=[
            pl.BlockSpec((tm, D_MODEL), lambda i: (i, 0)),
            pl.BlockSpec((tm, QKV_COLS), lambda i: (i, 0)),
        ],
        out_shape=[
            jax.ShapeDtypeStruct((t, D_MODEL), BF16),
            jax.ShapeDtypeStruct((t, QKV_COLS), BF16),
        ],
        compiler_params=_params(("parallel",)),
        name="rms_qkv",
    )(x2, g, w)


def _glu_sgu_kernel(h_ref, *refs):
    *w_refs, g_ref, b_ref, ws_ref, bst_ref, z_ref, o_ref = refs
    n = len(w_refs) // 4
    wa, wg, wu, wv = (w_refs[k * n:(k + 1) * n] for k in range(4))
    h = h_ref[...]
    for k in range(n):
        cols = slice(k * W_IN_COL_BLOCK, (k + 1) * W_IN_COL_BLOCK)
        z_ref[:, cols] = (_dot(h, wa[k][...]) * _sigmoid(_dot(h, wg[k][...]))).astype(BF16)
    project = lambda blocks: jax.nn.gelu(jnp.concatenate([_dot(h, w[...]) for w in blocks], axis=1))
    u = project(wu)
    v = project(wv)
    for grp in range(SG_GROUPS):
        cols = slice(grp * SG_GROUP_CH, (grp + 1) * SG_GROUP_CH)
        vn = _layer_norm(v[:, cols], g_ref[:, cols], b_ref[:, cols]).astype(BF16)
        w = ws_ref[grp].astype(BF16)
        bias = bst_ref[:, grp:grp + 1]
        for r0 in range(0, h.shape[0], SG_CHUNK):
            rows = slice(r0, r0 + SG_CHUNK)
            o_ref[rows, cols] = (u[rows, cols] * (_dot(w, vn[rows, :]) + bias)).astype(BF16)


def _glu_sgu_call(h, w_in_bf16, ln_g, ln_b, ws, bst, tm=1024):
    t = h.shape[0]
    assert tm % SG_CHUNK == 0 and CONV_CH == SG_CH
    first = QKV_COLS // W_IN_COL_BLOCK
    n_blocks = 2 * (CONV_CH + SG_CH) // W_IN_COL_BLOCK
    out_spec = pl.BlockSpec((tm, SG_CH), lambda i: (i, 0))
    return pl.pallas_call(
        _glu_sgu_kernel,
        grid=(t // tm,),
        in_specs=[
            pl.BlockSpec((tm, D_MODEL), lambda i: (i, 0)),
            *_w_in_cols(first, n_blocks),
            _resident((1, SG_CH)),
            _resident((1, SG_CH)),
            _resident((SG_GROUPS, SG_CHUNK, SG_CHUNK)),
            _resident((SG_CHUNK, SG_GROUPS)),
        ],
        out_specs=[out_spec, out_spec],
        out_shape=[jax.ShapeDtypeStruct((t, CONV_CH), BF16), jax.ShapeDtypeStruct((t, SG_CH), BF16)],
        compiler_params=_params(("parallel",)),
        name="glu_and_spatial_gating",
    )(h, *([w_in_bf16] * n_blocks), ln_g, ln_b, ws, bst)


def _gates_kernel(h_ref, *refs):
    *w_refs, b_ref, o_ref = refs
    h = h_ref[...]
    for k, w_ref in enumerate(w_refs):
        cols = slice(k * W_IN_COL_BLOCK, (k + 1) * W_IN_COL_BLOCK)
        o_ref[:, cols] = _sigmoid(_dot(h, w_ref[...]) + b_ref[:, cols]).astype(BF16)


def _gates_call(h, w_in_bf16, b, tm=1024):
    t = h.shape[0]
    n = N_BRANCH * D_MODEL
    first = (IN_COLS - n) // W_IN_COL_BLOCK
    n_blocks = n // W_IN_COL_BLOCK
    return pl.pallas_call(
        _gates_kernel,
        grid=(t // tm,),
        in_specs=[pl.BlockSpec((tm, D_MODEL), lambda i: (i, 0)), *_w_in_cols(first, n_blocks),
                  _resident((1, n))],
        out_specs=pl.BlockSpec((tm, n), lambda i: (i, 0)),
        out_shape=jax.ShapeDtypeStruct((t, n), BF16),
        compiler_params=_params(("parallel",)),
        name="branch_gates",
    )(h, *([w_in_bf16] * n_blocks), b)


def _norm_rope(y, cos, sin):
    y = y.astype(F32)
    rinv = lax.rsqrt(jnp.mean(y * y, axis=-1, keepdims=True) + RMS_EPS)
    return ((y * cos + pltpu.roll(y, ROPE_PAIRS, axis=1) * sin) * rinv).astype(BF16)


def _attn_kernel(q_ref, k_ref, v_ref, qcos_ref, qsin_ref, kcos_ref, ksin_ref, *rest, sub_rows, n_ride):
    ride_in, (o_ref, *ride_out), k_scr = rest[:n_ride], rest[n_ride:-1], rest[-1]
    for src, dst in zip(ride_in, ride_out):
        dst[...] = src[...].astype(BF16)

    @pl.when(pl.program_id(2) == 0)
    def _():
        k_scr[...] = _norm_rope(k_ref[...], kcos_ref[...], ksin_ref[...])

    k = k_scr[...]
    v = v_ref[...]
    v_ext = jnp.concatenate([v, jnp.ones_like(v)], axis=1)
    tq = q_ref.shape[0]
    for r0 in range(0, tq, sub_rows):
        rows = slice(r0, r0 + sub_rows)
        for grp in range(Q_PER_KV):
            lo = grp * HEAD_DIM
            q = _norm_rope(q_ref[rows, lo:lo + HEAD_DIM], qcos_ref[rows, :], qsin_ref[rows, :])
            s = lax.dot_general(q, k, (((1,), (1,)), ((), ())), preferred_element_type=F32)
            p = jnp.exp2(s - jnp.max(s, axis=-1, keepdims=True)).astype(BF16)
            o = _dot(p, v_ext)
            o_ref[rows, lo:lo + HEAD_DIM] = (o[:, :HEAD_DIM] / o[:, HEAD_DIM:]).astype(BF16)


def _attn_call(qkv, tables, ride, batch, seq, tq=1024):
    t = qkv.shape[0]
    q_blocks = seq // tq
    grp_cols = Q_PER_KV * HEAD_DIM
    k_col0 = Q_COLS // HEAD_DIM
    v_col0 = (Q_COLS + KV_COLS) // HEAD_DIM
    q_table = pl.BlockSpec((tq, HEAD_DIM), lambda b, h, i: (i, 0))
    k_table = pl.BlockSpec((seq, HEAD_DIM), lambda b, h, i: (0, 0))
    n_steps = batch * N_KV_HEADS * q_blocks
    step = lambda b, h, i: (b * N_KV_HEADS + h) * q_blocks + i
    ride_in, ride_out, ride_shapes = [], [], []
    for w, layer in ride:
        _, rows, cols = w.shape
        slab = rows // n_steps
        assert slab * n_steps == rows and slab % BF16_SUBLANES == 0
        ride_in.append(pl.BlockSpec((None, slab, cols), functools.partial(
            lambda layer, b, h, i: (layer, step(b, h, i), 0), layer)))
        ride_out.append(pl.BlockSpec((slab, cols), lambda b, h, i: (step(b, h, i), 0)))
        ride_shapes.append(jax.ShapeDtypeStruct((rows, cols), BF16))
    out = pl.pallas_call(
        functools.partial(_attn_kernel, sub_rows=256, n_ride=len(ride)),
        grid=(batch, N_KV_HEADS, q_blocks),
        in_specs=[
            pl.BlockSpec((tq, grp_cols), lambda b, h, i: (b * q_blocks + i, h)),
            pl.BlockSpec((seq, HEAD_DIM), lambda b, h, i: (b, k_col0 + h)),
            pl.BlockSpec((seq, HEAD_DIM), lambda b, h, i: (b, v_col0 + h)),
            q_table, q_table, k_table, k_table, *ride_in,
        ],
        out_specs=[pl.BlockSpec((tq, grp_cols), lambda b, h, i: (b * q_blocks + i, h)), *ride_out],
        out_shape=[jax.ShapeDtypeStruct((t, Q_COLS), BF16), *ride_shapes],
        scratch_shapes=[pltpu.VMEM((seq, HEAD_DIM), BF16)],
        compiler_params=_params(("arbitrary", "arbitrary", "arbitrary")),
        name="gqa_attention",
    )(qkv, qkv, qkv, *tables, *(w for w, _ in ride))
    return out[0], out[1:]


CONV_TILE_ROWS = 64
CONV_WIN_ROWS = CONV_TILE_ROWS + 2 * CONV_HALO


def _shift_matrix():
    n_shifts = V7X_SUBLANES - 1
    r = lax.broadcasted_iota(jnp.int32, (n_shifts * CONV_WIN_ROWS, CONV_WIN_ROWS), 0)
    c = lax.broadcasted_iota(jnp.int32, (n_shifts * CONV_WIN_ROWS, CONV_WIN_ROWS), 1)
    return (c == r % CONV_WIN_ROWS + r // CONV_WIN_ROWS + 1).astype(BF16)


def _conv_tile(win, shift_mat, w_ref, b_ref, half_g, half_beta, tile_ref):
    rows = CONV_TILE_ROWS
    shifted_all = _dot(shift_mat, win)
    win_f32 = win.astype(F32)
    for cb in range(CONV_CH // V7X_LANES):
        lanes = slice(cb * V7X_LANES, (cb + 1) * V7X_LANES)
        acc = jnp.zeros((rows, V7X_LANES), F32)
        for sub in range(V7X_SUBLANES):
            base = (sub - 1) * CONV_WIN_ROWS
            for grp in range(2 * CONV_HALO // V7X_SUBLANES):
                tap = V7X_SUBLANES * grp + sub - (CONV_HALO - CONV_PAD)
                if 0 <= tap < CONV_WIDTH:
                    lo = V7X_SUBLANES * grp
                    src = (win_f32[lo:lo + rows, lanes] if sub == 0
                           else shifted_all[base + lo:base + lo + rows, lanes])
                    acc = acc + src * w_ref[tap:tap + 1, lanes]
        tile_ref[:, lanes] = acc + b_ref[:, lanes]
    h = _layer_norm(tile_ref[...], half_g, half_beta)
    return (h + h * jnp.tanh(h)).astype(BF16)


def _merge_kernel(a_ref, zprev_ref, zcur_ref, znext_ref, s_ref, gate_ref, x_ref, cw_ref, cb_ref, cg_ref,
                  cbeta_ref, wa_ref, wc_ref, ws_ref, wo_ref, o_ref, win_ref, tile_ref, c_ref, *, seq):
    tm = x_ref.shape[0]
    r0 = pl.program_id(0) * tm
    zero = jnp.zeros((CONV_HALO, CONV_CH), BF16)
    win_ref[0:CONV_HALO, :] = jnp.where(r0 % seq == 0, zero, zprev_ref[...])
    win_ref[CONV_HALO:CONV_HALO + tm, :] = zcur_ref[...]
    win_ref[CONV_HALO + tm:, :] = jnp.where((r0 + tm) % seq == 0, zero, znext_ref[...])
    shift_mat = _shift_matrix()
    half_g = 0.5 * cg_ref[...]
    half_beta = 0.5 * cbeta_ref[...]
    for u, base in enumerate(range(0, tm, CONV_TILE_ROWS)):
        c_ref[base:base + CONV_TILE_ROWS, :] = _conv_tile(
            win_ref[base:base + CONV_WIN_ROWS, :], shift_mat, cw_ref, cb_ref, half_g, half_beta, tile_ref.at[u])

    merged = gate_ref[:, 0:D_MODEL].astype(F32) * _dot(a_ref[...], wa_ref[...])
    merged = merged + gate_ref[:, 2 * D_MODEL:].astype(F32) * _dot(s_ref[...], ws_ref[...])
    merged = merged + gate_ref[:, D_MODEL:2 * D_MODEL].astype(F32) * _dot(c_ref[...], wc_ref[...])
    o_ref[...] = x_ref[...] + _dot(merged.astype(BF16), wo_ref[...])


def _merge_call(attn, z, sgu, gates, x2, w_dw, b_dw, ln_g, ln_b, wa, wc, ws, wo, seq, tm=512):
    t = x2.shape[0]
    half = D_MODEL // 2
    assert seq % tm == 0 and tm % CONV_TILE_ROWS == 0
    halo_per_block = tm // CONV_HALO
    last_halo = t // CONV_HALO - 1
    return pl.pallas_call(
        functools.partial(_merge_kernel, seq=seq),
        grid=(t // tm,),
        in_specs=[
            pl.BlockSpec((tm, half), lambda i: (i, 0)),
            pl.BlockSpec((CONV_HALO, CONV_CH), lambda i: (jnp.maximum(i * halo_per_block - 1, 0), 0)),
            pl.BlockSpec((tm, CONV_CH), lambda i: (i, 0)),
            pl.BlockSpec((CONV_HALO, CONV_CH), lambda i: (jnp.minimum((i + 1) * halo_per_block, last_halo), 0)),
            pl.BlockSpec((tm, half), lambda i: (i, 0)),
            pl.BlockSpec((tm, N_BRANCH * D_MODEL), lambda i: (i, 0)),
            pl.BlockSpec((tm, D_MODEL), lambda i: (i, 0)),
            _resident((CONV_WIDTH, CONV_CH)),
            _resident((1, CONV_CH)),
            _resident((1, CONV_CH)),
            _resident((1, CONV_CH)),
            _resident((half, D_MODEL)),
            _resident((half, D_MODEL)),
            _resident((half, D_MODEL)),
            _resident((D_MODEL, D_MODEL)),
        ],
        out_specs=pl.BlockSpec((tm, D_MODEL), lambda i: (i, 0)),
        out_shape=jax.ShapeDtypeStruct((t, D_MODEL), F32),
        scratch_shapes=[pltpu.VMEM((tm + 2 * CONV_HALO, CONV_CH), BF16),
                        pltpu.VMEM((tm // CONV_TILE_ROWS, CONV_TILE_ROWS, CONV_CH), F32),
                        pltpu.VMEM((tm, CONV_CH), BF16)],
        compiler_params=_params(("parallel",)),
        name="conv_merge_out_proj",
    )(attn, z, z, z, sgu, gates, x2, w_dw, b_dw, ln_g, ln_b, wa, wc, ws, wo)


def _ffn_kernel(x_ref, g_ref, wg_ref, wu_ref, wd_ref, gf_ref, o_ref, h_ref, *, final_norm):
    j = pl.program_id(1)

    @pl.when(j == 0)
    def _():
        x = x_ref[...]
        h_ref[...] = _rms(x, g_ref[...]).astype(BF16)
        o_ref[...] = x

    h = h_ref[...]
    a = _dot(h, wg_ref[...])
    b = _dot(h, wu_ref[...])
    mid = (_silu(a) * b).astype(BF16)
    o_ref[...] += _dot(mid, wd_ref[...])

    if final_norm:
        @pl.when(j == pl.num_programs(1) - 1)
        def _():
            o_ref[...] = _rms(o_ref[...], gf_ref[...])


def _ffn_call(x2, g, wg, wu, wd, g_final, final_norm, tm=1024, tf=512):
    t = x2.shape[0]
    d_ff = wg.shape[1]
    return pl.pallas_call(
        functools.partial(_ffn_kernel, final_norm=final_norm),
        grid=(t // tm, d_ff // tf),
        in_specs=[
            pl.BlockSpec((tm, D_MODEL), lambda i, j: (i, 0)),
            _resident((1, D_MODEL)),
            pl.BlockSpec((D_MODEL, tf), lambda i, j: (0, j)),
            pl.BlockSpec((D_MODEL, tf), lambda i, j: (0, j)),
            pl.BlockSpec((tf, D_MODEL), lambda i, j: (j, 0)),
            _resident((1, D_MODEL)),
        ],
        out_specs=pl.BlockSpec((tm, D_MODEL), lambda i, j: (i, 0)),
        out_shape=jax.ShapeDtypeStruct((t, D_MODEL), F32),
        scratch_shapes=[pltpu.VMEM((tm, D_MODEL), BF16)],
        compiler_params=_params(("parallel", "arbitrary")),
        name="swiglu_ffn",
    )(x2, g, wg, wu, wd, g_final)


def _rope_tables(seq):
    pos = jnp.arange(seq, dtype=jnp.int32)
    row = (pos // GRID_W).astype(F32)
    col = (pos % GRID_W).astype(F32)
    inv = ROPE_THETA ** (-jnp.arange(ROPE_FREQ_PER_AXIS, dtype=F32) / ROPE_FREQ_PER_AXIS)
    ang = jnp.concatenate([row[:, None] * inv, col[:, None] * inv], axis=-1)
    cos, sin = jnp.cos(ang), jnp.sin(ang)
    return jnp.concatenate([cos, cos], axis=-1), jnp.concatenate([-sin, sin], axis=-1)


def _gained_tables(cos2, sin2, gain, scale):
    return cos2 * (gain * scale), sin2 * (jnp.roll(gain, ROPE_PAIRS) * scale)


def kernel(x, g_mix, w_in, b_gate, q_norm_g, k_norm_g, w_attn_o, w_dw, b_dw, conv_ln_g, conv_ln_b, w_conv_o, sg_ln_g, sg_ln_b, w_s, b_s, w_sg_o, w_out, g_ffn, w_ff_gate, w_ff_up, w_ff_down, g_final):
    batch, seq, _ = x.shape
    depth = w_in.shape[0]
    cos2, sin2 = _rope_tables(seq)
    x2 = x.reshape(batch * seq, D_MODEL)
    row = lambda p: p.reshape(1, -1)
    w_in_bf16 = _cast_call(w_in, 0)
    for l in range(depth):
        h, qkv = _qkv_call(x2, row(g_mix[l]), w_in_bf16)
        z, sgu = _glu_sgu_call(h, w_in_bf16, row(sg_ln_g[l]), row(sg_ln_b[l]), w_s[l], b_s[l].T)
        gates = _gates_call(h, w_in_bf16, row(b_gate[l]))
        tables = (*_gained_tables(cos2, sin2, q_norm_g[l], HEAD_DIM ** -0.5 * LOG2_E),
                  *_gained_tables(cos2, sin2, k_norm_g[l], 1.0))
        ride = [(w, l) for w in (w_attn_o, w_conv_o, w_sg_o, w_out, w_ff_gate, w_ff_up, w_ff_down)]
        if l + 1 < depth:
            ride.append((w_in, l + 1))
        attn, staged = _attn_call(qkv, tables, ride, batch, seq)
        w_a, w_c, w_s_o, w_o, w_fg, w_fu, w_fd = staged[:7]
        x2 = _merge_call(attn, z, sgu, gates, x2, w_dw[l].reshape(CONV_WIDTH, CONV_CH), row(b_dw[l]),
                         row(conv_ln_g[l]), row(conv_ln_b[l]), w_a, w_c, w_s_o, w_o, seq)
        x2 = _ffn_call(x2, row(g_ffn[l]), w_fg, w_fu, w_fd, row(g_final), final_norm=(l == depth - 1))
        if l + 1 < depth:
            w_in_bf16 = staged[7]
    return x2.reshape(batch, seq, D_MODEL)
```
